```python
import math
import jax, jax.numpy as jnp
from jax import lax
import numpy as np

D_MODEL = 1024
BATCH = 4
SEQ = 8192
DEPTH = 2

EPS = 1e-6
MIX_WIDTH = 512
N_BRANCH = 4
RET_HEADS = 4
RET_DK = MIX_WIDTH // RET_HEADS
RET_DV = MIX_WIDTH // RET_HEADS
RET_CHUNK = 128
ROPE_BASE = 10000.0
CONV_CH = MIX_WIDTH
CONV_WIDTH = 31
GMLP_CH = MIX_WIDTH
GMLP_GROUPS = 4
GMLP_GROUP_CH = GMLP_CH // GMLP_GROUPS
GMLP_CHUNK = 128
SWA_HEADS = 8
SWA_KV_HEADS = 2
SWA_GROUP = SWA_HEADS // SWA_KV_HEADS
SWA_HEAD_DIM = MIX_WIDTH // SWA_HEADS
SWA_WINDOW = 128
SWA_BLOCK = 128
REL_BUCKETS = 32
REL_MAX_DIST = 128
D_FF = ((8 * D_MODEL + 3 * 256 - 1) // (3 * 256)) * 256
RET_COLS = 4 * MIX_WIDTH
CONV_COLS = 2 * CONV_CH
GMLP_COLS = 2 * GMLP_CH
SWA_COLS = (SWA_HEADS + 2 * SWA_KV_HEADS) * SWA_HEAD_DIM
GATE_COLS = N_BRANCH * D_MODEL
IN_COLS = RET_COLS + CONV_COLS + GMLP_COLS + SWA_COLS + GATE_COLS
IN_SPLITS = [RET_COLS, RET_COLS + CONV_COLS, RET_COLS + CONV_COLS + GMLP_COLS,
             RET_COLS + CONV_COLS + GMLP_COLS + SWA_COLS]

kernel_name = "hybrid_gated_parallel_mixers"


def rms_norm(x, g):
    xf = x.astype(jnp.float32)
    y = xf * lax.rsqrt(jnp.mean(xf * xf, axis=-1, keepdims=True) + EPS)
    return (y * g.astype(jnp.float32)).astype(x.dtype)


def layer_norm(x, g, b):
    xf = x.astype(jnp.float32)
    mu = jnp.mean(xf, axis=-1, keepdims=True)
    var = jnp.mean(jnp.square(xf - mu), axis=-1, keepdims=True)
    return (xf - mu) * lax.rsqrt(var + EPS) * g.astype(jnp.float32) + b.astype(jnp.float32)


def rotary(x, pos):
    half = x.shape[-1] // 2
    inv = ROPE_BASE ** (-jnp.arange(half, dtype=jnp.float32) / half)
    ang = pos.astype(jnp.float32)[:, None] * inv[None, :]
    cos = jnp.cos(ang)[None, :, None, :]
    sin = jnp.sin(ang)[None, :, None, :]
    x1, x2 = x[..., :half], x[..., half:]
    return jnp.concatenate([x1 * cos - x2 * sin, x1 * sin + x2 * cos], axis=-1)


def retention(za, gn_g):
    B, S, _ = za.shape
    C = RET_CHUNK
    N = S // C
    H = RET_HEADS
    qa, ka, va, ga = jnp.split(za, 4, axis=-1)
    pos = jnp.arange(S)
    q = rotary(qa.astype(jnp.float32).reshape(B, S, H, RET_DK), pos)
    k = rotary(ka.astype(jnp.float32).reshape(B, S, H, RET_DK), pos) * (RET_DK ** -0.5)
    v = va.astype(jnp.float32).reshape(B, S, H, RET_DV)
    gamma = 1.0 - 2.0 ** (-5.0 - jnp.arange(H, dtype=jnp.float32))
    log_g = jnp.log(gamma)
    idx = jnp.arange(C, dtype=jnp.float32)
    diff = idx[:, None] - idx[None, :]
    decay = jnp.where(diff >= 0, jnp.exp(log_g[:, None, None] * jnp.maximum(diff, 0.0)), 0.0)
    xi = jnp.exp(log_g[None, :] * (idx[:, None] + 1.0))
    zeta = jnp.exp(log_g[None, :] * (C - 1.0 - idx[:, None]))
    chunk_decay = jnp.exp(log_g * C)
    qc = q.reshape(B, N, C, H, RET_DK)
    kc = k.reshape(B, N, C, H, RET_DK)
    vc = v.reshape(B, N, C, H, RET_DV)
    scores = jnp.einsum('bnthd,bnshd->bnhts', qc, kc) * decay[None, None]
    y_inner = jnp.einsum('bnhts,bnshv->bnthv', scores, vc)
    kv = jnp.einsum('bnshd,bnshv->bnhdv', kc * zeta[None, None, :, :, None], vc)

    def step(state, kv_n):
        return chunk_decay[None, :, None, None] * state + kv_n, state

    state0 = jnp.zeros((B, H, RET_DK, RET_DV), jnp.float32)
    _, state_prev = lax.scan(step, state0, jnp.moveaxis(kv, 1, 0))
    state_prev = jnp.moveaxis(state_prev, 0, 1)
    y_cross = jnp.einsum('bnthd,bnhdv->bnthv', qc * xi[None, None, :, :, None], state_prev)
    y = (y_inner + y_cross).reshape(B, S, H, RET_DV)
    mu = jnp.mean(y, axis=-1, keepdims=True)
    var = jnp.mean(jnp.square(y - mu), axis=-1, keepdims=True)
    y = ((y - mu) * lax.rsqrt(var + EPS)).reshape(B, S, H * RET_DV) * gn_g.astype(jnp.float32)
    return jax.nn.silu(ga.astype(jnp.float32)) * y


def conformer_conv(zb, conv_w, conv_b, ln_g, ln_b):
    a, gate = jnp.split(zb, 2, axis=-1)
    u = (a * jax.nn.sigmoid(gate)).astype(conv_w.dtype)
    u = lax.conv_general_dilated(u, conv_w, window_strides=(1,), padding=[(CONV_WIDTH - 1, 0)],
                                 dimension_numbers=('NWC', 'WIO', 'NWC'),
                                 feature_group_count=CONV_CH)
    u = u + conv_b
    return jax.nn.silu(layer_norm(u, ln_g, ln_b))


def spatial_gating(zc, ln_g, ln_b, w_s, b_s):
    B, S, _ = zc.shape
    N = S // GMLP_CHUNK
    hc = jax.nn.gelu(zc.astype(jnp.float32))
    u, v = jnp.split(hc, 2, axis=-1)
    v = layer_norm(v, ln_g, ln_b).reshape(B, N, GMLP_CHUNK, GMLP_GROUPS, GMLP_GROUP_CH)
    mask = jnp.tril(jnp.ones((GMLP_CHUNK, GMLP_CHUNK), jnp.float32))
    ws = w_s.astype(jnp.float32) * mask[None]
    s = jnp.einsum('gts,bnsgc->bntgc', ws, v) + b_s.astype(jnp.float32).T[None, None, :, :, None]
    return u * s.reshape(B, S, GMLP_CH)


def t5_bucket(dist):
    max_exact = REL_BUCKETS // 2
    d = jnp.maximum(dist, 1).astype(jnp.float32)
    large = max_exact + (jnp.log(d / max_exact) / math.log(REL_MAX_DIST / max_exact)
                         * (REL_BUCKETS - max_exact)).astype(jnp.int32)
    large = jnp.minimum(large, REL_BUCKETS - 1)
    return jnp.where(dist < max_exact, dist, large)


def swa_attention(zd, q_g, k_g, sinks, rel_bias):
    B, S, _ = zd.shape
    Bk = SWA_BLOCK
    N = S // Bk
    d = SWA_HEAD_DIM
    qd, kd, vd = jnp.split(zd, [SWA_HEADS * d, (SWA_HEADS + SWA_KV_HEADS) * d], axis=-1)
    q = rms_norm(qd.reshape(B, S, SWA_HEADS, d), q_g).astype(jnp.float32)
    k = rms_norm(kd.reshape(B, S, SWA_KV_HEADS, d), k_g).astype(jnp.float32)
    v = vd.reshape(B, S, SWA_KV_HEADS, d).astype(jnp.float32)
    qb = q.reshape(B, N, Bk, SWA_KV_HEADS, SWA_GROUP, d)

    def band(t):
        tb = t.reshape(B, N, Bk, SWA_KV_HEADS, d)
        prev = jnp.pad(tb[:, :-1], ((0, 0), (1, 0), (0, 0), (0, 0), (0, 0)))
        return jnp.concatenate([prev, tb], axis=2)

    kb, vb = band(k), band(v)
    s = jnp.einsum('bnqhgd,bnkhd->bnhgqk', qb, kb) * (d ** -0.5)
    qi = jnp.arange(Bk)[:, None] + Bk
    kj = jnp.arange(2 * Bk)[None, :]
    dist = qi - kj
    in_win = (dist >= 0) & (dist < SWA_WINDOW)
    bias = rel_bias.astype(jnp.float32)[t5_bucket(jnp.maximum(dist, 0))]
    bias = bias.transpose(2, 0, 1).reshape(SWA_KV_HEADS, SWA_GROUP, Bk, 2 * Bk)
    valid = in_win[None] & ((jnp.arange(N)[:, None, None] > 0) | (kj[None] >= Bk))
    s = jnp.where(valid[None, :, None, None], s + bias[None, None], -jnp.inf)
    sink = sinks.astype(jnp.float32).reshape(SWA_KV_HEADS, SWA_GROUP)[None, None, :, :, None, None]
    m = jnp.maximum(jnp.max(s, axis=-1, keepdims=True), sink)
    p = jnp.exp(s - m)
    denom = jnp.sum(p, axis=-1, keepdims=True) + jnp.exp(sink - m)
    o = jnp.einsum('bnhgqk,bnkhd->bnqhgd', p / denom, vb)
    return o.reshape(B, S, SWA_HEADS * d)


def hybrid_layer(x, rel_bias, norm1_g, w_in, ret_gn_g, conv_w, conv_b, conv_ln_g, conv_ln_b,
                 gmlp_ln_g, gmlp_ln_b, gmlp_ws, gmlp_bs, swa_q_g, swa_k_g, swa_sinks,
                 w_branch, w_out, norm2_g, w_ffn_in, w_ffn_out):
    B, S, _ = x.shape
    h = rms_norm(x, norm1_g)
    z = h @ w_in
    za, zb, zc, zd, zg = jnp.split(z, IN_SPLITS, axis=-1)
    ya = retention(za, ret_gn_g)
    yb = conformer_conv(zb, conv_w, conv_b, conv_ln_g, conv_ln_b)
    yc = spatial_gating(zc, gmlp_ln_g, gmlp_ln_b, gmlp_ws, gmlp_bs)
    yd = swa_attention(zd, swa_q_g, swa_k_g, swa_sinks, rel_bias)
    ys = jnp.stack([ya, yb, yc, yd], axis=2).astype(w_branch.dtype)
    branches = jnp.einsum('bsnc,ncd->bsnd', ys, w_branch)
    gates = jax.nn.sigmoid(zg.reshape(B, S, N_BRANCH, D_MODEL))
    mix = jnp.sum(gates * branches, axis=2)
    x = x + (mix @ w_out).astype(x.dtype)
    h = rms_norm(x, norm2_g)
    gt, up = jnp.split(h @ w_ffn_in, 2, axis=-1)
    return x + ((jax.nn.silu(gt) * up) @ w_ffn_out).astype(x.dtype)


def setup_inputs(seed: int = 0) -> dict:
    key = jax.random.key(seed)
    ks = jax.random.split(key, 24)
    f32 = jnp.float32
    nrm = lambda k, shape, s: jax.random.normal(k, shape, f32) * s
    L = DEPTH
    return {
        'x': nrm(ks[0], (BATCH, SEQ, D_MODEL), 1.0),
        'rel_bias': nrm(ks[1], (REL_BUCKETS, SWA_HEADS), 0.5),
        'norm1_g': 1.0 + nrm(ks[2], (L, D_MODEL), 0.02),
        'w_in': nrm(ks[3], (L, D_MODEL, IN_COLS), D_MODEL ** -0.5),
        'ret_gn_g': 1.0 + nrm(ks[4], (L, MIX_WIDTH), 0.02),
        'conv_w': nrm(ks[5], (L, CONV_WIDTH, 1, CONV_CH), CONV_WIDTH ** -0.5),
        'conv_b': nrm(ks[6], (L, CONV_CH), 0.02),
        'conv_ln_g': 1.0 + nrm(ks[7], (L, CONV_CH), 0.02),
        'conv_ln_b': nrm(ks[8], (L, CONV_CH), 0.02),
        'gmlp_ln_g': 1.0 + nrm(ks[9], (L, GMLP_CH), 0.02),
        'gmlp_ln_b': nrm(ks[10], (L, GMLP_CH), 0.02),
        'gmlp_ws': nrm(ks[11], (L, GMLP_GROUPS, GMLP_CHUNK, GMLP_CHUNK), 0.5 * GMLP_CHUNK ** -0.5),
        'gmlp_bs': 1.0 + nrm(ks[12], (L, GMLP_GROUPS, GMLP_CHUNK), 0.02),
        'swa_q_g': 1.0 + nrm(ks[13], (L, SWA_HEAD_DIM), 0.02),
        'swa_k_g': 1.0 + nrm(ks[14], (L, SWA_HEAD_DIM), 0.02),
        'swa_sinks': nrm(ks[15], (L, SWA_HEADS), 0.5),
        'w_branch': nrm(ks[16], (L, N_BRANCH, MIX_WIDTH, D_MODEL), MIX_WIDTH ** -0.5),
        'w_out': nrm(ks[17], (L, D_MODEL, D_MODEL), 0.5 * D_MODEL ** -0.5),
        'norm2_g': 1.0 + nrm(ks[18], (L, D_MODEL), 0.02),
        'w_ffn_in': nrm(ks[19], (L, D_MODEL, 2 * D_FF), D_MODEL ** -0.5),
        'w_ffn_out': nrm(ks[20], (L, D_FF, D_MODEL), 0.5 * D_FF ** -0.5),
    }


def reference(x, rel_bias, norm1_g, w_in, ret_gn_g, conv_w, conv_b, conv_ln_g, conv_ln_b,
              gmlp_ln_g, gmlp_ln_b, gmlp_ws, gmlp_bs, swa_q_g, swa_k_g, swa_sinks,
              w_branch, w_out, norm2_g, w_ffn_in, w_ffn_out):
    for l in range(DEPTH):
        x = hybrid_layer(x, rel_bias, norm1_g[l], w_in[l], ret_gn_g[l], conv_w[l], conv_b[l],
                         conv_ln_g[l], conv_ln_b[l], gmlp_ln_g[l], gmlp_ln_b[l], gmlp_ws[l],
                         gmlp_bs[l], swa_q_g[l], swa_k_g[l], swa_sinks[l], w_branch[l],
                         w_out[l], norm2_g[l], w_ffn_in[l], w_ffn_out[l])
    return x
```

```python
import functools
import math

import jax
import jax.numpy as jnp
from jax import lax
from jax.experimental import pallas as pl
from jax.experimental.pallas import tpu as pltpu

D_MODEL = 1024
SEQ = 8192
EPS = 1e-6
MIX = 512
N_BRANCH = 4
CHUNK = 128
RET_HEADS = 4
RET_DK = 128
ROPE_BASE = 10000.0
CONV_WIDTH = 31
CONV_TAIL = 32
GMLP_GROUPS = 4
SWA_HEADS = 8
SWA_KV_HEADS = 2
SWA_GROUP = SWA_HEADS // SWA_KV_HEADS
SWA_HEAD_DIM = 64
SWA_WINDOW = 128
REL_BUCKETS = 32
REL_MAX_DIST = 128
D_FF = 2816
OFF_RET = 0
OFF_CONV = 2048
OFF_GMLP = 3072
OFF_SWA = 4096
OFF_GATE = 4864
IN_COLS = 8960

MIXER_TILE = 256
FFN_TILE = 256
VMEM_LIMIT_BYTES = 56 * 1024 * 1024

F32 = jnp.float32
BF16 = jnp.bfloat16


def _mm(a, b):
    return jnp.dot(a, b, preferred_element_type=F32)


def _mm_nt(a, b):
    return lax.dot_general(a, b, (((1,), (1,)), ((), ())), preferred_element_type=F32)


def _sigmoid(x):
    return 0.5 * jnp.tanh(0.5 * x) + 0.5


def _silu(x):
    return x * _sigmoid(x)


def _gelu_tanh(x):
    c = math.sqrt(2.0 / math.pi)
    return x * (0.5 * (1.0 + jnp.tanh(c * (x + 0.044715 * (x * x * x)))))


def _rms_norm(x, g):
    return x * lax.rsqrt(jnp.mean(x * x, axis=-1, keepdims=True) + EPS) * g


def _layer_norm(x, g, b):
    mu = jnp.mean(x, axis=-1, keepdims=True)
    d = x - mu
    var = jnp.mean(d * d, axis=-1, keepdims=True)
    return d * lax.rsqrt(var + EPS) * g + b


def _mixer_kernel(x_ref, cq_ref, sq_ref, ck_ref, sk_ref, decay_ref, xi_ref, zeta_ref, cdec_ref,
                  n1g_ref, win_ref, gng_ref, cw_ref, cb_ref, clg_ref, clb_ref, glg_ref, glb_ref,
                  ws_ref, bs_ref, qg_ref, kg_ref, sinks_ref, bias_ref, hsum_ref, wbr_ref, wout_ref,
                  o_ref, state_ref, ubuf_ref, kprev_ref, vprev_ref):
    ts = x_ref.shape[1]
    n_chunks = ts // CHUNK
    t = pl.program_id(1)

    @pl.when(t == 0)
    def _():
        state_ref[...] = jnp.zeros_like(state_ref)
        ubuf_ref[0:CONV_TAIL, :] = jnp.zeros((CONV_TAIL, MIX), F32)
        kprev_ref[...] = jnp.zeros_like(kprev_ref)
        vprev_ref[...] = jnp.zeros_like(vprev_ref)

    x = x_ref[0]
    hb = _rms_norm(x, n1g_ref[...]).astype(BF16)

    za = _mm(hb, win_ref[:, OFF_RET:OFF_RET + 2048])
    gng = gng_ref[...]
    cdec = cdec_ref[...]
    ya_chunks = []
    for c in range(n_chunks):
        r = slice(c * CHUNK, (c + 1) * CHUNK)
        cq, sq, ck, sk = cq_ref[r, :], sq_ref[r, :], ck_ref[r, :], sk_ref[r, :]
        st = state_ref[...]
        heads = []
        new_state = []
        for hd in range(RET_HEADS):
            l = slice(hd * RET_DK, (hd + 1) * RET_DK)
            q = za[r, hd * 128:(hd + 1) * 128]
            k = za[r, 512 + hd * 128:512 + (hd + 1) * 128]
            v = za[r, 1024 + hd * 128:1024 + (hd + 1) * 128]
            q = q * cq + pltpu.roll(q, 64, 1) * sq
            k = k * ck + pltpu.roll(k, 64, 1) * sk
            vb = v.astype(BF16)
            s = _mm_nt(q.astype(BF16), k.astype(BF16)) * decay_ref[hd]
            lhs = jnp.concatenate([s.astype(BF16), (q * xi_ref[:, l]).astype(BF16)], axis=1)
            rhs = jnp.concatenate([vb, st[:, l].astype(BF16)], axis=0)
            y = _mm(lhs, rhs)
            kzt = (k * zeta_ref[:, l]).T.astype(BF16)
            new_state.append(cdec[:, l] * st[:, l] + _mm(kzt, vb))
            mu = jnp.mean(y, axis=-1, keepdims=True)
            d = y - mu
            var = jnp.mean(d * d, axis=-1, keepdims=True)
            heads.append(d * lax.rsqrt(var + EPS))
        state_ref[...] = jnp.concatenate(new_state, axis=1)
        yn = jnp.concatenate(heads, axis=1) * gng
        ya_chunks.append(_silu(za[r, 1536:2048]) * yn)
    ya = jnp.concatenate(ya_chunks, axis=0)

    zb = _mm(hb, win_ref[:, OFF_CONV:OFF_CONV + 1024])
    ubuf_ref[CONV_TAIL:CONV_TAIL + ts, :] = zb[:, :MIX] * _sigmoid(zb[:, MIX:])
    base = CONV_TAIL - (CONV_WIDTH - 1)
    acc = jnp.zeros((ts, MIX), F32) + cb_ref[...]
    for b in range(8):
        n_a = (CONV_WIDTH - b + 7) // 8
        sb = ubuf_ref[base + b:base + b + ts + 8 * (n_a - 1), :]
        for a in range(n_a):
            j = 8 * a + b
            acc = acc + cw_ref[j:j + 1, :] * sb[8 * a:8 * a + ts, :]
    ubuf_ref[0:CONV_TAIL, :] = ubuf_ref[ts:ts + CONV_TAIL, :]
    yb = _silu(_layer_norm(acc, clg_ref[...], clb_ref[...]))

    zc = _gelu_tanh(_mm(hb, win_ref[:, OFF_GMLP:OFF_GMLP + 1024]))
    ug = zc[:, :MIX]
    vg = _layer_norm(zc[:, MIX:], glg_ref[...], glb_ref[...]).astype(BF16)
    row = lax.broadcasted_iota(jnp.int32, (CHUNK, CHUNK), 0)
    col = lax.broadcasted_iota(jnp.int32, (CHUNK, CHUNK), 1)
    tril = row >= col
    s_groups = []
    for g in range(GMLP_GROUPS):
        wsm = jnp.where(tril, ws_ref[g], 0.0).astype(BF16)
        rhs = jnp.concatenate([vg[c * CHUNK:(c + 1) * CHUNK, g * 128:(g + 1) * 128]
                               for c in range(n_chunks)], axis=1)
        sg = _mm(wsm, rhs)
        bsg = bs_ref[g]
        s_groups.append(jnp.concatenate([sg[:, c * 128:(c + 1) * 128] + bsg
                                         for c in range(n_chunks)], axis=0))
    yc = ug * jnp.concatenate(s_groups, axis=1)

    zd = _mm(hb, win_ref[:, OFF_SWA:OFF_SWA + 768])
    qd, kd, vd = zd[:, :512], zd[:, 512:640], zd[:, 640:768]
    hsum = hsum_ref[...]
    qn = qd * lax.rsqrt(_mm((qd * qd).astype(BF16), hsum) + EPS) * qg_ref[...]
    kn = kd * lax.rsqrt(_mm((kd * kd).astype(BF16), hsum[0:128, 0:128]) + EPS) * kg_ref[...]
    lane_lo = lax.broadcasted_iota(jnp.int32, (1, 128), 1) < SWA_HEAD_DIM
    first = jnp.where(t == 0, -jnp.inf, 0.0)
    yd_chunks = []
    for c in range(n_chunks):
        r = slice(c * CHUNK, (c + 1) * CHUNK)
        kc, vc = kn[r, :], vd[r, :]
        kcat = jnp.concatenate([kprev_ref[...], kc], axis=0)
        vcat = jnp.concatenate([vprev_ref[...], vc], axis=0)
        kprev_ref[...] = kc
        vprev_ref[...] = vc
        kroll = pltpu.roll(kcat, 64, 1)
        vroll = pltpu.roll(vcat, 64, 1)
        kplace = [[jnp.where(lane_lo, kcat, 0.0).astype(BF16), jnp.where(lane_lo, 0.0, kroll).astype(BF16)],
                  [jnp.where(lane_lo, kroll, 0.0).astype(BF16), jnp.where(lane_lo, 0.0, kcat).astype(BF16)]]
        vplace = [[jnp.where(lane_lo, vcat, 0.0).astype(BF16), jnp.where(lane_lo, 0.0, vroll).astype(BF16)],
                  [jnp.where(lane_lo, vroll, 0.0).astype(BF16), jnp.where(lane_lo, 0.0, vcat).astype(BF16)]]
        outs = []
        for j in range(SWA_HEADS // 2):
            kv = (2 * j) // SWA_GROUP
            qblk = qn[r, j * 128:(j + 1) * 128].astype(BF16)
            ps, rinv = [], []
            for pos in range(2):
                hidx = 2 * j + pos
                s = _mm_nt(qblk, kplace[kv][pos]) + bias_ref[hidx]
                if c == 0:
                    s = jnp.concatenate([s[:, :CHUNK] + first, s[:, CHUNK:]], axis=1)
                sink = sinks_ref[hidx]
                m = jnp.maximum(jnp.max(s, axis=-1, keepdims=True), sink)
                p = jnp.exp(s - m)
                den = jnp.sum(p, axis=-1, keepdims=True) + jnp.exp(sink - m)
                ps.append(p.astype(BF16))
                rinv.append(1.0 / den)
            o = _mm(jnp.concatenate(ps, axis=1),
                    jnp.concatenate([vplace[kv][0], vplace[kv][1]], axis=0))
            outs.append(o * jnp.where(lane_lo, rinv[0], rinv[1]))
        yd_chunks.append(jnp.concatenate(outs, axis=1))
    yd = jnp.concatenate(yd_chunks, axis=0)

    mix = None
    for b, y in enumerate((ya, yb, yc, yd)):
        gate = _sigmoid(_mm(hb, win_ref[:, OFF_GATE + b * D_MODEL:OFF_GATE + (b + 1) * D_MODEL]))
        term = gate * _mm(y.astype(BF16), wbr_ref[b])
        mix = term if mix is None else mix + term
    o_ref[0] = x + _mm(mix.astype(BF16), wout_ref[...])


def _ffn_kernel(x_ref, g_ref, win_ref, wout_ref, o_ref):
    x = x_ref[...]
    hb = _rms_norm(x, g_ref[...]).astype(BF16)
    gt = _mm(hb, win_ref[:, :D_FF])
    up = _mm(hb, win_ref[:, D_FF:])
    o_ref[...] = x + _mm((_silu(gt) * up).astype(BF16), wout_ref[...])


def _const_spec(shape):
    nd = len(shape)
    return pl.BlockSpec(shape, lambda *_: (0,) * nd, pipeline_mode=pl.Buffered(1))


def _t5_bucket(dist):
    max_exact = REL_BUCKETS // 2
    d = jnp.maximum(dist, 1).astype(F32)
    large = max_exact + (jnp.log(d / max_exact) / math.log(REL_MAX_DIST / max_exact)
                         * (REL_BUCKETS - max_exact)).astype(jnp.int32)
    large = jnp.minimum(large, REL_BUCKETS - 1)
    return jnp.where(dist < max_exact, dist, large)


def _position_tables():
    half = RET_DK // 2
    inv = ROPE_BASE ** (-jnp.arange(half, dtype=F32) / half)
    ang = jnp.arange(SEQ).astype(F32)[:, None] * inv[None, :]
    cos, sin = jnp.cos(ang), jnp.sin(ang)
    cos2 = jnp.concatenate([cos, cos], axis=1)
    sin2 = jnp.concatenate([-sin, sin], axis=1)
    kscale = RET_DK ** -0.5
    gamma = 1.0 - 2.0 ** (-5.0 - jnp.arange(RET_HEADS, dtype=F32))
    log_g = jnp.log(gamma)
    idx = jnp.arange(CHUNK, dtype=F32)
    diff = idx[:, None] - idx[None, :]
    decay = jnp.where(diff >= 0, jnp.exp(log_g[:, None, None] * jnp.maximum(diff, 0.0)), 0.0)
    xi = jnp.repeat(jnp.exp(log_g[None, :] * (idx[:, None] + 1.0)), RET_DK, axis=1)
    zeta = jnp.repeat(jnp.exp(log_g[None, :] * (CHUNK - 1.0 - idx[:, None])), RET_DK, axis=1)
    cdec = jnp.repeat(jnp.exp(log_g * CHUNK), RET_DK)[None, :]
    return cos2, sin2, cos2 * kscale, sin2 * kscale, decay, xi, zeta, cdec


def _attention_bias(rel_bias):
    qi = jnp.arange(CHUNK)[:, None] + CHUNK
    kj = jnp.arange(2 * CHUNK)[None, :]
    dist = qi - kj
    in_win = (dist >= 0) & (dist < SWA_WINDOW)
    bias = rel_bias.astype(F32)[_t5_bucket(jnp.maximum(dist, 0))]
    return jnp.where(in_win[None], bias.transpose(2, 0, 1), -jnp.inf)


def _mixer_call(x, tables, bias, hsum, n1g, win, gng, cw, cb, clg, clb, glg, glb, ws, bs, qg, kg, sinks,
                wbr, wout):
    B, S, D = x.shape
    ts = MIXER_TILE
    cos2, sin2, cosk, sink_t, decay, xi, zeta, cdec = tables
    pos_spec = pl.BlockSpec((ts, 128), lambda b, t: (t, 0))
    in_specs = [
        pl.BlockSpec((1, ts, D), lambda b, t: (b, t, 0)),
        pos_spec, pos_spec, pos_spec, pos_spec,
        _const_spec(decay.shape), _const_spec(xi.shape), _const_spec(zeta.shape), _const_spec(cdec.shape),
        _const_spec(n1g.shape), _const_spec(win.shape), _const_spec(gng.shape),
        _const_spec(cw.shape), _const_spec(cb.shape), _const_spec(clg.shape), _const_spec(clb.shape),
        _const_spec(glg.shape), _const_spec(glb.shape), _const_spec(ws.shape), _const_spec(bs.shape),
        _const_spec(qg.shape), _const_spec(kg.shape),
        pl.BlockSpec(memory_space=pltpu.SMEM),
        _const_spec(bias.shape), _const_spec(hsum.shape), _const_spec(wbr.shape), _const_spec(wout.shape),
    ]
    return pl.pallas_call(
        _mixer_kernel,
        grid=(B, S // ts),
        in_specs=in_specs,
        out_specs=pl.BlockSpec((1, ts, D), lambda b, t: (b, t, 0)),
        out_shape=jax.ShapeDtypeStruct(x.shape, x.dtype),
        scratch_shapes=[
            pltpu.VMEM((RET_DK, MIX), F32),
            pltpu.VMEM((CONV_TAIL + ts, MIX), F32),
            pltpu.VMEM((CHUNK, 128), F32),
            pltpu.VMEM((CHUNK, 128), F32),
        ],
        compiler_params=pltpu.CompilerParams(
            dimension_semantics=("arbitrary", "arbitrary"),
            vmem_limit_bytes=VMEM_LIMIT_BYTES),
        name="mixer",
    )(x, cos2, sin2, cosk, sink_t, decay, xi, zeta, cdec, n1g, win, gng, cw, cb, clg, clb, glg, glb,
      ws, bs, qg, kg, sinks, bias, hsum, wbr, wout)


def _ffn_call(x2, g, win, wout):
    T, D = x2.shape
    ts = FFN_TILE
    return pl.pallas_call(
        _ffn_kernel,
        grid=(T // ts,),
        in_specs=[pl.BlockSpec((ts, D), lambda i: (i, 0)),
                  _const_spec(g.shape), _const_spec(win.shape), _const_spec(wout.shape)],
        out_specs=pl.BlockSpec((ts, D), lambda i: (i, 0)),
        out_shape=jax.ShapeDtypeStruct(x2.shape, x2.dtype),
        compiler_params=pltpu.CompilerParams(
            dimension_semantics=("arbitrary",),
            vmem_limit_bytes=VMEM_LIMIT_BYTES),
        name="ffn",
    )(x2, g, win, wout)


def kernel(x, rel_bias, norm1_g, w_in, ret_gn_g, conv_w, conv_b, conv_ln_g, conv_ln_b, gmlp_ln_g, gmlp_ln_b, gmlp_ws, gmlp_bs, swa_q_g, swa_k_g, swa_sinks, w_branch, w_out, norm2_g, w_ffn_in, w_ffn_out):
    B, S, D = x.shape
    depth = w_in.shape[0]
    tables = _position_tables()
    bias = _attention_bias(rel_bias)
    head_id = jnp.arange(MIX) // SWA_HEAD_DIM
    hsum = jnp.where(head_id[:, None] == head_id[None, :], 1.0 / SWA_HEAD_DIM, 0.0).astype(BF16)
    row = lambda v: v.reshape(1, -1).astype(F32)
    for l in range(depth):
        x = _mixer_call(
            x, tables, bias, hsum,
            row(norm1_g[l]), w_in[l].astype(BF16), row(ret_gn_g[l]),
            conv_w[l].reshape(CONV_WIDTH, MIX).astype(F32), row(conv_b[l]),
            row(conv_ln_g[l]), row(conv_ln_b[l]), row(gmlp_ln_g[l]), row(gmlp_ln_b[l]),
            gmlp_ws[l].astype(F32),
            jnp.broadcast_to(gmlp_bs[l].astype(F32)[:, :, None], (GMLP_GROUPS, CHUNK, 128)),
            row(jnp.tile(swa_q_g[l], SWA_HEADS)) * (SWA_HEAD_DIM ** -0.5),
            row(jnp.tile(swa_k_g[l], SWA_KV_HEADS)),
            swa_sinks[l].astype(F32),
            w_branch[l].astype(BF16), w_out[l].astype(BF16))
        x = _ffn_call(x.reshape(B * S, D), row(norm2_g[l]), w_ffn_in[l].astype(BF16),
                      w_ffn_out[l].astype(BF16)).reshape(B, S, D)
    return x
```

```python
import functools
import math

import jax
import jax.numpy as jnp
from jax import lax
from jax.experimental import pallas as pl
from jax.experimental.pallas import tpu as pltpu

D_MODEL = 1024
SEQ = 8192
EPS = 1e-6
MIX = 512
N_BRANCH = 4
CHUNK = 128
RET_HEADS = 4
RET_DK = 128
ROPE_BASE = 10000.0
CONV_WIDTH = 31
CONV_TAIL = 32
CONV_ROWS = 32
SUBLANES = 8
GMLP_GROUPS = 4
SWA_HEADS = 8
SWA_KV_HEADS = 2
SWA_GROUP = SWA_HEADS // SWA_KV_HEADS
SWA_HEAD_DIM = 64
SWA_WINDOW = 128
REL_BUCKETS = 32
REL_MAX_DIST = 128
D_FF = 2816
OFF_RET = 0
OFF_CONV = 2048
OFF_GMLP = 3072
OFF_SWA = 4096
OFF_GATE = 4864
IN_COLS = 8960

MIXER_TILE = 256
FFN_TILE = 256
COL_BLOCK = 256
VMEM_LIMIT_BYTES = 58 * 1024 * 1024

F32 = jnp.float32
BF16 = jnp.bfloat16


def _mm(a, b):
    return jnp.dot(a, b, preferred_element_type=F32)


def _mm_nt(a, b):
    return lax.dot_general(a, b, (((1,), (1,)), ((), ())), preferred_element_type=F32)


def _sigmoid(x):
    return 0.5 * jnp.tanh(0.5 * x) + 0.5


def _silu(x):
    return x * _sigmoid(x)


def _gelu_tanh(x):
    c = math.sqrt(2.0 / math.pi)
    return x * (0.5 * (1.0 + jnp.tanh(c * (x + 0.044715 * (x * x * x)))))


def _rms_norm(x, g):
    return x * lax.rsqrt(jnp.mean(x * x, axis=-1, keepdims=True) + EPS) * g


def _layer_norm(x, g, b):
    mu = jnp.mean(x, axis=-1, keepdims=True)
    d = x - mu
    var = jnp.mean(d * d, axis=-1, keepdims=True)
    return d * lax.rsqrt(var + EPS) * g + b


def _interleave(*streams):
    keyed = []
    for s, stream in enumerate(streams):
        keyed += [((i + 0.5) / len(stream), s, i, fn) for i, fn in enumerate(stream)]
    for _, _, _, fn in sorted(keyed, key=lambda e: e[:3]):
        fn()


def _mixer_kernel(layer, x_ref, cq_ref, sq_ref, ck_ref, sk_ref, decay_ref, xi_ref, zeta_ref, cdec_ref,
                  n1g_ref, win_ref, gng_ref, cw_ref, cb_ref, clg_ref, clb_ref, glg_ref, glb_ref,
                  ws_ref, bs_ref, qg_ref, kg_ref, sinks_ref, rel_ref, bucket_ref, hsum_ref, wbr_ref, wout_ref,
                  o_ref, state_ref, ubuf_ref, ushift_ref, yb_ref, gates_ref, kprev_ref, vprev_ref, bias_ref):
    ts = x_ref.shape[1]
    n_chunks = ts // CHUNK
    t = pl.program_id(1)

    @pl.when((pl.program_id(0) == 0) & (t == 0))
    def _():
        bucket = bucket_ref[...]
        for h in range(SWA_HEADS):
            bias_ref[h] = lax.fori_loop(
                0, REL_BUCKETS, lambda k, acc: jnp.where(bucket == k, rel_ref[k, h], acc),
                jnp.full(bucket.shape, -jnp.inf, F32))

    @pl.when(t == 0)
    def _():
        state_ref[...] = jnp.zeros_like(state_ref)
        ubuf_ref[0:CONV_TAIL, :] = jnp.zeros((CONV_TAIL, MIX), F32)
        kprev_ref[...] = jnp.zeros_like(kprev_ref)
        vprev_ref[...] = jnp.zeros_like(vprev_ref)

    x = x_ref[0]
    hb = _rms_norm(x, n1g_ref[...]).astype(BF16)
    z = {}

    def in_proj(off, width):
        return _mm(hb, win_ref[:, off:off + width])

    def proj_task(name, off, width):
        def run():
            z[name] = in_proj(off, width)
        return run

    def gate_task(i):
        def run():
            c0 = i * COL_BLOCK
            gates_ref[:, c0:c0 + COL_BLOCK] = _sigmoid(in_proj(OFF_GATE + c0, COL_BLOCK))
        return run

    gate_tasks = [gate_task(i) for i in range(N_BRANCH * D_MODEL // COL_BLOCK)]

    mix = [None] * (D_MODEL // COL_BLOCK)

    def branch_tasks(b, get_y):
        def task(j):
            def run():
                if j == 0:
                    z["ybf", b] = get_y().astype(BF16)
                c0 = j * COL_BLOCK
                term = (gates_ref[:, b * D_MODEL + c0:b * D_MODEL + c0 + COL_BLOCK]
                        * _mm(z["ybf", b], wbr_ref[b, :, c0:c0 + COL_BLOCK]))
                mix[j] = term if mix[j] is None else mix[j] + term
            return run
        return [task(j) for j in range(D_MODEL // COL_BLOCK)]

    zb = in_proj(OFF_CONV, 2 * MIX)
    ubuf_ref[CONV_TAIL:CONV_TAIL + ts, :] = zb[:, :MIX] * _sigmoid(zb[:, MIX:])
    base = CONV_TAIL - (CONV_WIDTH - 1)
    taps = []
    shift_tasks = []
    for b in range(SUBLANES):
        n_a = (CONV_WIDTH - b + SUBLANES - 1) // SUBLANES
        off = base + b
        if off % SUBLANES == 0:
            src, src_off = ubuf_ref, off
        else:
            def shift(b=b, off=off, rows=ts + SUBLANES * (n_a - 1)):
                ushift_ref[b, 0:rows, :] = ubuf_ref[off:off + rows, :]
            shift_tasks.append(shift)
            src, src_off = ushift_ref.at[b], 0
        taps += [(SUBLANES * a + b, src, src_off + SUBLANES * a) for a in range(n_a)]

    _interleave(shift_tasks,
                [proj_task("qk", OFF_RET, 1024), proj_task("vg", OFF_RET + 1024, 1024),
                 proj_task("c", OFF_GMLP, 1024), proj_task("d", OFF_SWA, 768)])

    def gmlp_front():
        zc = _gelu_tanh(z["c"])
        z["ug"] = zc[:, :MIX]
        z["vg16"] = _layer_norm(zc[:, MIX:], glg_ref[...], glb_ref[...]).astype(BF16)

    def swa_front():
        zd = z["d"]
        qd, kd = zd[:, :512], zd[:, 512:640]
        hsum = hsum_ref[...]
        z["qn"] = qd * lax.rsqrt(_mm((qd * qd).astype(BF16), hsum) + EPS) * qg_ref[...]
        z["kn"] = kd * lax.rsqrt(_mm((kd * kd).astype(BF16), hsum[0:128, 0:128]) + EPS) * kg_ref[...]
        z["vd"] = zd[:, 640:768]

    _interleave([gmlp_front, swa_front], gate_tasks[0:4])

    cb, clg, clb = cb_ref[...], clg_ref[...], clb_ref[...]

    def conv_task(r0):
        def run():
            acc = cb
            for j, src, off in taps:
                acc = acc + cw_ref[j:j + 1, :] * src[off + r0:off + r0 + CONV_ROWS, :]
            yb_ref[r0:r0 + CONV_ROWS, :] = _silu(_layer_norm(acc, clg, clb)).astype(BF16)
        return run

    _interleave([conv_task(r0) for r0 in range(0, ts, CONV_ROWS)], gate_tasks[4:12])
    ubuf_ref[0:CONV_TAIL, :] = ubuf_ref[ts:ts + CONV_TAIL, :]

    gng = gng_ref[...]
    cdec = cdec_ref[...]
    ya_chunks = []
    ret = {}

    def ret_task(c, hd):
        def run():
            r = slice(c * CHUNK, (c + 1) * CHUNK)
            l = slice(hd * RET_DK, (hd + 1) * RET_DK)
            if hd == 0:
                ret["st"] = state_ref[...]
                ret["heads"], ret["new_state"] = [], []
            st = ret["st"]
            q = z["qk"][r, hd * 128:(hd + 1) * 128]
            k = z["qk"][r, 512 + hd * 128:512 + (hd + 1) * 128]
            v = z["vg"][r, hd * 128:(hd + 1) * 128]
            q = q * cq_ref[r, :] + pltpu.roll(q, 64, 1) * sq_ref[r, :]
            k = k * ck_ref[r, :] + pltpu.roll(k, 64, 1) * sk_ref[r, :]
            vb = v.astype(BF16)
            s = _mm_nt(q.astype(BF16), k.astype(BF16)) * decay_ref[hd]
            lhs = jnp.concatenate([s.astype(BF16), (q * xi_ref[:, l]).astype(BF16)], axis=1)
            rhs = jnp.concatenate([vb, st[:, l].astype(BF16)], axis=0)
            y = _mm(lhs, rhs)
            kzt = (k * zeta_ref[:, l]).T.astype(BF16)
            ret["new_state"].append(cdec[:, l] * st[:, l] + _mm(kzt, vb))
            mu = jnp.mean(y, axis=-1, keepdims=True)
            d = y - mu
            var = jnp.mean(d * d, axis=-1, keepdims=True)
            ret["heads"].append(d * lax.rsqrt(var + EPS))
            if hd == RET_HEADS - 1:
                state_ref[...] = jnp.concatenate(ret["new_state"], axis=1)
                yn = jnp.concatenate(ret["heads"], axis=1) * gng
                ya_chunks.append(_silu(z["vg"][r, 512:1024]) * yn)
        return run

    tril = (lax.broadcasted_iota(jnp.int32, (CHUNK, CHUNK), 0)
            >= lax.broadcasted_iota(jnp.int32, (CHUNK, CHUNK), 1))
    s_groups = []

    def gmlp_task(g):
        def run():
            wsm = jnp.where(tril, ws_ref[g], 0.0).astype(BF16)
            vg = z["vg16"]
            rhs = jnp.concatenate([vg[c * CHUNK:(c + 1) * CHUNK, g * 128:(g + 1) * 128]
                                   for c in range(n_chunks)], axis=1)
            sg = _mm(wsm, rhs)
            bsg = bs_ref[g]
            s_groups.append(jnp.concatenate([sg[:, c * 128:(c + 1) * 128] + bsg
                                             for c in range(n_chunks)], axis=0))
        return run

    _interleave([ret_task(c, hd) for c in range(n_chunks) for hd in range(RET_HEADS)],
                gate_tasks[12:16] + [gmlp_task(g) for g in range(GMLP_GROUPS)])

    lane_lo = lax.broadcasted_iota(jnp.int32, (1, 128), 1) < SWA_HEAD_DIM
    first = jnp.where(t == 0, -jnp.inf, 0.0)
    yd_chunks = []
    att = {}

    def attn_task(c, j):
        def run():
            r = slice(c * CHUNK, (c + 1) * CHUNK)
            if j == 0:
                kc, vc = z["kn"][r, :], z["vd"][r, :]
                kcat = jnp.concatenate([kprev_ref[...], kc], axis=0)
                vcat = jnp.concatenate([vprev_ref[...], vc], axis=0)
                kprev_ref[...] = kc
                vprev_ref[...] = vc
                kroll = pltpu.roll(kcat, 64, 1)
                vroll = pltpu.roll(vcat, 64, 1)
                att["k"] = [[jnp.where(lane_lo, kcat, 0.0).astype(BF16), jnp.where(lane_lo, 0.0, kroll).astype(BF16)],
                            [jnp.where(lane_lo, kroll, 0.0).astype(BF16), jnp.where(lane_lo, 0.0, kcat).astype(BF16)]]
                att["v"] = [[jnp.where(lane_lo, vcat, 0.0).astype(BF16), jnp.where(lane_lo, 0.0, vroll).astype(BF16)],
                            [jnp.where(lane_lo, vroll, 0.0).astype(BF16), jnp.where(lane_lo, 0.0, vcat).astype(BF16)]]
                att["outs"] = []
            kv = (2 * j) // SWA_GROUP
            qblk = z["qn"][r, j * 128:(j + 1) * 128].astype(BF16)
            ps, rinv = [], []
            for pos in range(2):
                hidx = 2 * j + pos
                s = _mm_nt(qblk, att["k"][kv][pos]) + bias_ref[hidx]
                if c == 0:
                    s = jnp.concatenate([s[:, :CHUNK] + first, s[:, CHUNK:]], axis=1)
                sink = sinks_ref[layer, hidx]
                m = jnp.maximum(jnp.max(s, axis=-1, keepdims=True), sink)
                p = jnp.exp(s - m)
                den = jnp.sum(p, axis=-1, keepdims=True) + jnp.exp(sink - m)
                ps.append(p.astype(BF16))
                rinv.append(1.0 / den)
            o = _mm(jnp.concatenate(ps, axis=1),
                    jnp.concatenate([att["v"][kv][0], att["v"][kv][1]], axis=0))
            att["outs"].append(o * jnp.where(lane_lo, rinv[0], rinv[1]))
            if j == SWA_HEADS // 2 - 1:
                yd_chunks.append(jnp.concatenate(att["outs"], axis=1))
        return run

    _interleave([attn_task(c, j) for c in range(n_chunks) for j in range(SWA_HEADS // 2)],
                branch_tasks(0, lambda: jnp.concatenate(ya_chunks, axis=0))
                + branch_tasks(1, lambda: yb_ref[...])
                + branch_tasks(2, lambda: z["ug"] * jnp.concatenate(s_groups, axis=1)))
    for task in branch_tasks(3, lambda: jnp.concatenate(yd_chunks, axis=0)):
        task()

    o_ref[0] = x + _mm(jnp.concatenate(mix, axis=1).astype(BF16), wout_ref[...])


def _ffn_kernel(x_ref, g_ref, win_ref, wout_ref, o_ref):
    x = x_ref[...]
    hb = _rms_norm(x, g_ref[...]).astype(BF16)
    gt = _mm(hb, win_ref[:, :D_FF])
    up = _mm(hb, win_ref[:, D_FF:])
    o_ref[...] = x + _mm((_silu(gt) * up).astype(BF16), wout_ref[...])


def _const_spec(shape):
    nd = len(shape)
    return pl.BlockSpec(shape, lambda *_: (0,) * nd, pipeline_mode=pl.Buffered(1))


def _layer_spec(arr, layer):
    nd = arr.ndim - 1
    return pl.BlockSpec((None,) + arr.shape[1:], lambda *_: (layer,) + (0,) * nd, pipeline_mode=pl.Buffered(1))


def _t5_bucket(dist):
    max_exact = REL_BUCKETS // 2
    d = jnp.maximum(dist, 1).astype(F32)
    large = max_exact + (jnp.log(d / max_exact) / math.log(REL_MAX_DIST / max_exact)
                         * (REL_BUCKETS - max_exact)).astype(jnp.int32)
    large = jnp.minimum(large, REL_BUCKETS - 1)
    return jnp.where(dist < max_exact, dist, large)


def _position_tables():
    half = RET_DK // 2
    inv = ROPE_BASE ** (-jnp.arange(half, dtype=F32) / half)
    ang = jnp.arange(SEQ).astype(F32)[:, None] * inv[None, :]
    cos, sin = jnp.cos(ang), jnp.sin(ang)
    cos2 = jnp.concatenate([cos, cos], axis=1)
    sin2 = jnp.concatenate([-sin, sin], axis=1)
    kscale = RET_DK ** -0.5
    gamma = 1.0 - 2.0 ** (-5.0 - jnp.arange(RET_HEADS, dtype=F32))
    log_g = jnp.log(gamma)
    idx = jnp.arange(CHUNK, dtype=F32)
    diff = idx[:, None] - idx[None, :]
    decay = jnp.where(diff >= 0, jnp.exp(log_g[:, None, None] * jnp.maximum(diff, 0.0)), 0.0)
    xi = jnp.repeat(jnp.exp(log_g[None, :] * (idx[:, None] + 1.0)), RET_DK, axis=1)
    zeta = jnp.repeat(jnp.exp(log_g[None, :] * (CHUNK - 1.0 - idx[:, None])), RET_DK, axis=1)
    cdec = jnp.repeat(jnp.exp(log_g * CHUNK), RET_DK)[None, :]
    return cos2, sin2, cos2 * kscale, sin2 * kscale, decay, xi, zeta, cdec


def _bucket_table():
    qi = jnp.arange(CHUNK)[:, None] + CHUNK
    kj = jnp.arange(2 * CHUNK)[None, :]
    dist = qi - kj
    in_win = (dist >= 0) & (dist < SWA_WINDOW)
    return jnp.where(in_win, _t5_bucket(jnp.maximum(dist, 0)), -1).astype(jnp.int32)


def _mixer_call(layer, x, tables, bucket, hsum, rel_bias, sinks, layer_params):
    B, S, D = x.shape
    ts = MIXER_TILE
    pos_spec = pl.BlockSpec((ts, 128), lambda b, t: (t, 0))
    smem = pl.BlockSpec(memory_space=pltpu.SMEM)
    n1g, win, gng, cw, cb, clg, clb, glg, glb, ws, bs, qg, kg, wbr, wout = layer_params
    lspec = lambda a: _layer_spec(a, layer)
    in_specs = (
        [pl.BlockSpec((1, ts, D), lambda b, t: (b, t, 0))]
        + [pos_spec] * 4 + [_const_spec(a.shape) for a in tables[4:]]
        + [lspec(a) for a in (n1g, win, gng, cw, cb, clg, clb, glg, glb, ws, bs, qg, kg)]
        + [smem, smem, _const_spec(bucket.shape), _const_spec(hsum.shape), lspec(wbr), lspec(wout)])
    return pl.pallas_call(
        functools.partial(_mixer_kernel, layer),
        grid=(B, S // ts),
        in_specs=in_specs,
        out_specs=pl.BlockSpec((1, ts, D), lambda b, t: (b, t, 0)),
        out_shape=jax.ShapeDtypeStruct(x.shape, x.dtype),
        scratch_shapes=[
            pltpu.VMEM((RET_DK, MIX), F32),
            pltpu.VMEM((CONV_TAIL + ts, MIX), F32),
            pltpu.VMEM((SUBLANES, ts + CONV_TAIL - SUBLANES, MIX), F32),
            pltpu.VMEM((ts, MIX), BF16),
            pltpu.VMEM((ts, N_BRANCH * D_MODEL), F32),
            pltpu.VMEM((CHUNK, 128), F32),
            pltpu.VMEM((CHUNK, 128), F32),
            pltpu.VMEM((SWA_HEADS, CHUNK, 2 * CHUNK), F32),
        ],
        compiler_params=pltpu.CompilerParams(
            dimension_semantics=("arbitrary", "arbitrary"),
            vmem_limit_bytes=VMEM_LIMIT_BYTES),
        name="mixer",
    )(x, *tables, n1g, win, gng, cw, cb, clg, clb, glg, glb, ws, bs, qg, kg,
      sinks, rel_bias, bucket, hsum, wbr, wout)


def _ffn_call(layer, x2, g, win, wout):
    T, D = x2.shape
    ts = FFN_TILE
    return pl.pallas_call(
        _ffn_kernel,
        grid=(T // ts,),
        in_specs=[pl.BlockSpec((ts, D), lambda i: (i, 0)),
                  _layer_spec(g, layer), _layer_spec(win, layer), _layer_spec(wout, layer)],
        out_specs=pl.BlockSpec((ts, D), lambda i: (i, 0)),
        out_shape=jax.ShapeDtypeStruct(x2.shape, x2.dtype),
        compiler_params=pltpu.CompilerParams(
            dimension_semantics=("arbitrary",),
            vmem_limit_bytes=VMEM_LIMIT_BYTES),
        name="ffn",
    )(x2, g, win, wout)


def kernel(x, rel_bias, norm1_g, w_in, ret_gn_g, conv_w, conv_b, conv_ln_g, conv_ln_b, gmlp_ln_g, gmlp_ln_b, gmlp_ws, gmlp_bs, swa_q_g, swa_k_g, swa_sinks, w_branch, w_out, norm2_g, w_ffn_in, w_ffn_out):
    B, S, D = x.shape
    depth = w_in.shape[0]
    tables = _position_tables()
    bucket = _bucket_table()
    head_id = jnp.arange(MIX) // SWA_HEAD_DIM
    hsum = jnp.where(head_id[:, None] == head_id[None, :], 1.0 / SWA_HEAD_DIM, 0.0).astype(BF16)
    rows = lambda v: v.reshape(depth, 1, -1).astype(F32)
    layer_params = (
        rows(norm1_g), w_in.astype(BF16), rows(ret_gn_g),
        conv_w.reshape(depth, CONV_WIDTH, MIX).astype(F32), rows(conv_b),
        rows(conv_ln_g), rows(conv_ln_b), rows(gmlp_ln_g), rows(gmlp_ln_b),
        gmlp_ws.astype(F32),
        jnp.broadcast_to(gmlp_bs.astype(F32)[..., None], (depth, GMLP_GROUPS, CHUNK, 128)),
        rows(jnp.tile(swa_q_g, (1, SWA_HEADS))) * (SWA_HEAD_DIM ** -0.5),
        rows(jnp.tile(swa_k_g, (1, SWA_KV_HEADS))),
        w_branch.astype(BF16), w_out.astype(BF16))
    n2g, wfi, wfo = rows(norm2_g), w_ffn_in.astype(BF16), w_ffn_out.astype(BF16)
    for l in range(depth):
        x = _mixer_call(l, x, tables, bucket, hsum, rel_bias.astype(F32), swa_sinks.astype(F32), layer_params)
        x = _ffn_call(l, x.reshape(B * S, D), n2g, wfi, wfo).reshape(B, S, D)
    return x
```

```python
import functools
import math

import jax
import jax.numpy as jnp
from jax import lax
from jax.experimental import pallas as pl
from jax.experimental.pallas import tpu as pltpu

D_MODEL = 1024
SEQ = 8192
EPS = 1e-6
MIX = 512
N_BRANCH = 4
CHUNK = 128
RET_HEADS = 4
RET_DK = 128
ROPE_BASE = 10000.0
CONV_WIDTH = 31
CONV_TAIL = 32
CONV_ROWS = 32
SUBLANES = 8
GMLP_GROUPS = 4
SWA_HEADS = 8
SWA_KV_HEADS = 2
SWA_GROUP = SWA_HEADS // SWA_KV_HEADS
SWA_HEAD_DIM = 64
SWA_WINDOW = 128
REL_BUCKETS = 32
REL_MAX_DIST = 128
D_FF = 2816
OFF_RET = 0
OFF_CONV = 2048
OFF_GMLP = 3072
OFF_SWA = 4096
OFF_GATE = 4864
IN_COLS = 8960

MIXER_TILE = 256
FFN_TILE = 512
COL_BLOCK = 256
VMEM_LIMIT_BYTES = 58 * 1024 * 1024

F32 = jnp.float32
BF16 = jnp.bfloat16


def _mm(a, b):
    return jnp.dot(a, b, preferred_element_type=F32)


def _mm_nt(a, b):
    return lax.dot_general(a, b, (((1,), (1,)), ((), ())), preferred_element_type=F32)


def _sigmoid(x):
    return 0.5 * jnp.tanh(0.5 * x) + 0.5


def _silu(x):
    return x * _sigmoid(x)


def _silu_of_half(h):
    return h * jnp.tanh(h) + h


def _gelu_tanh(x):
    c = math.sqrt(2.0 / math.pi)
    return x * (0.5 * (1.0 + jnp.tanh(c * (x + 0.044715 * (x * x * x)))))


def _rms_norm(x, g):
    return x * lax.rsqrt(jnp.mean(x * x, axis=-1, keepdims=True) + EPS) * g


def _layer_norm(x, g, b):
    mu = jnp.mean(x, axis=-1, keepdims=True)
    d = x - mu
    var = jnp.mean(d * d, axis=-1, keepdims=True)
    return d * lax.rsqrt(var + EPS) * g + b


def _interleave(*streams):
    keyed = []
    for s, stream in enumerate(streams):
        keyed += [((i + 0.5) / len(stream), s, i, fn) for i, fn in enumerate(stream)]
    for _, _, _, fn in sorted(keyed, key=lambda e: e[:3]):
        fn()


def _mixer_kernel(layer, x_ref, cq_ref, sq_ref, ck_ref, sk_ref, decay_ref, xi_ref, zeta_ref, cdec_ref,
                  n1g_ref, win_ref, gng_ref, cw_ref, cb_ref, clg_ref, clb_ref, glg_ref, glb_ref,
                  ws_ref, bs_ref, qg_ref, kg_ref, sinks_ref, rel_ref, bucket_ref, hsum_ref, wbr_ref, wout_ref,
                  o_ref, state_ref, ubuf_ref, ushift_ref, yb_ref, gates_ref, kprev_ref, vprev_ref, bias_ref):
    ts = x_ref.shape[1]
    n_chunks = ts // CHUNK
    t = pl.program_id(1)

    @pl.when((pl.program_id(0) == 0) & (t == 0))
    def _():
        bucket = bucket_ref[...]
        for h in range(SWA_HEADS):
            bias_ref[h] = lax.fori_loop(
                0, REL_BUCKETS, lambda k, acc: jnp.where(bucket == k, rel_ref[k, h], acc),
                jnp.full(bucket.shape, -jnp.inf, F32))

    @pl.when(t == 0)
    def _():
        state_ref[...] = jnp.zeros_like(state_ref)
        ubuf_ref[0:CONV_TAIL, :] = jnp.zeros((CONV_TAIL, MIX), F32)
        kprev_ref[...] = jnp.zeros_like(kprev_ref)
        vprev_ref[...] = jnp.zeros_like(vprev_ref)

    x = x_ref[0]
    hb = _rms_norm(x, n1g_ref[...]).astype(BF16)
    z = {}

    def in_proj(off, width):
        return _mm(hb, win_ref[:, off:off + width])

    def proj_task(name, off, width):
        def run():
            z[name] = in_proj(off, width)
        return run

    def gate_task(i):
        def run():
            c0 = i * COL_BLOCK
            gates_ref[:, c0:c0 + COL_BLOCK] = jnp.tanh(in_proj(OFF_GATE + c0, COL_BLOCK))
        return run

    gate_tasks = [gate_task(i) for i in range(N_BRANCH * D_MODEL // COL_BLOCK)]

    mix = [None] * (D_MODEL // COL_BLOCK)

    def branch_tasks(b, get_y):
        def task(j):
            def run():
                if j == 0:
                    z["ybf", b] = get_y().astype(BF16)
                c0 = j * COL_BLOCK
                br = _mm(z["ybf", b], wbr_ref[b, :, c0:c0 + COL_BLOCK])
                term = gates_ref[:, b * D_MODEL + c0:b * D_MODEL + c0 + COL_BLOCK] * br + br
                mix[j] = term if mix[j] is None else mix[j] + term
            return run
        return [task(j) for j in range(D_MODEL // COL_BLOCK)]

    zb = in_proj(OFF_CONV, 2 * MIX)
    ubuf_ref[CONV_TAIL:CONV_TAIL + ts, :] = zb[:, :MIX] * jnp.tanh(zb[:, MIX:]) + zb[:, :MIX]
    base = CONV_TAIL - (CONV_WIDTH - 1)
    taps = []
    shift_tasks = []
    for b in range(SUBLANES):
        n_a = (CONV_WIDTH - b + SUBLANES - 1) // SUBLANES
        off = base + b
        if off % SUBLANES == 0:
            src, src_off = ubuf_ref, off
        else:
            def shift(b=b, off=off, rows=ts + SUBLANES * (n_a - 1)):
                ushift_ref[b, 0:rows, :] = ubuf_ref[off:off + rows, :]
            shift_tasks.append(shift)
            src, src_off = ushift_ref.at[b], 0
        taps += [(SUBLANES * a + b, src, src_off + SUBLANES * a) for a in range(n_a)]

    _interleave(shift_tasks,
                [proj_task("qk", OFF_RET, 1024), proj_task("vg", OFF_RET + 1024, 1024),
                 proj_task("c", OFF_GMLP, 1024), proj_task("d", OFF_SWA, 768)])

    def gmlp_front():
        zc = _gelu_tanh(z["c"])
        z["ug"] = zc[:, :MIX]
        z["vg16"] = _layer_norm(zc[:, MIX:], glg_ref[...], glb_ref[...]).astype(BF16)

    def swa_front():
        zd = z["d"]
        qd, kd = zd[:, :512], zd[:, 512:640]
        hsum = hsum_ref[...]
        z["qn"] = qd * lax.rsqrt(_mm((qd * qd).astype(BF16), hsum) + EPS) * qg_ref[...]
        z["kn"] = kd * lax.rsqrt(_mm((kd * kd).astype(BF16), hsum[0:128, 0:128]) + EPS) * kg_ref[...]
        z["vd"] = zd[:, 640:768]

    _interleave([gmlp_front, swa_front], gate_tasks[0:4])

    cb, clg, clb = cb_ref[...], clg_ref[...], clb_ref[...]

    def conv_task(r0):
        def run():
            acc = cb
            for j, src, off in taps:
                w = jnp.concatenate([cw_ref[j]] * (CONV_ROWS // SUBLANES), axis=0)
                acc = acc + w * src[off + r0:off + r0 + CONV_ROWS, :]
            yb_ref[r0:r0 + CONV_ROWS, :] = _silu(_layer_norm(acc, clg, clb)).astype(BF16)
        return run

    _interleave([conv_task(r0) for r0 in range(0, ts, CONV_ROWS)], gate_tasks[4:12])
    ubuf_ref[0:CONV_TAIL, :] = ubuf_ref[ts:ts + CONV_TAIL, :]

    gng = gng_ref[...]
    cdec = cdec_ref[...]
    ya_chunks = []
    ret = {}

    def ret_task(c, hd):
        def run():
            r = slice(c * CHUNK, (c + 1) * CHUNK)
            l = slice(hd * RET_DK, (hd + 1) * RET_DK)
            if hd == 0:
                ret["st"] = state_ref[...]
                ret["heads"], ret["new_state"] = [], []
            st = ret["st"]
            q = z["qk"][r, hd * 128:(hd + 1) * 128]
            k = z["qk"][r, 512 + hd * 128:512 + (hd + 1) * 128]
            v = z["vg"][r, hd * 128:(hd + 1) * 128]
            q = q * cq_ref[r, :] + pltpu.roll(q, 64, 1) * sq_ref[r, :]
            k = k * ck_ref[r, :] + pltpu.roll(k, 64, 1) * sk_ref[r, :]
            vb = v.astype(BF16)
            s = _mm_nt(q.astype(BF16), k.astype(BF16)) * decay_ref[hd]
            lhs = jnp.concatenate([s.astype(BF16), (q * xi_ref[:, l]).astype(BF16)], axis=1)
            rhs = jnp.concatenate([vb, st[:, l].astype(BF16)], axis=0)
            y = _mm(lhs, rhs)
            kzt = (k * zeta_ref[:, l]).T.astype(BF16)
            ret["new_state"].append(cdec[:, l] * st[:, l] + _mm(kzt, vb))
            mu = jnp.mean(y, axis=-1, keepdims=True)
            d = y - mu
            var = jnp.mean(d * d, axis=-1, keepdims=True)
            ret["heads"].append(d * lax.rsqrt(var + EPS))
            if hd == RET_HEADS - 1:
                state_ref[...] = jnp.concatenate(ret["new_state"], axis=1)
                yn = jnp.concatenate(ret["heads"], axis=1) * gng
                ya_chunks.append(_silu_of_half(z["vg"][r, 512:1024]) * yn)
        return run

    tril = (lax.broadcasted_iota(jnp.int32, (CHUNK, CHUNK), 0)
            >= lax.broadcasted_iota(jnp.int32, (CHUNK, CHUNK), 1))
    s_groups = []

    def gmlp_task(g):
        def run():
            wsm = jnp.where(tril, ws_ref[g], 0.0).astype(BF16)
            vg = z["vg16"]
            rhs = jnp.concatenate([vg[c * CHUNK:(c + 1) * CHUNK, g * 128:(g + 1) * 128]
                                   for c in range(n_chunks)], axis=1)
            sg = _mm(wsm, rhs)
            bsg = bs_ref[g]
            s_groups.append(jnp.concatenate([sg[:, c * 128:(c + 1) * 128] + bsg
                                             for c in range(n_chunks)], axis=0))
        return run

    _interleave([ret_task(c, hd) for c in range(n_chunks) for hd in range(RET_HEADS)],
                gate_tasks[12:16] + [gmlp_task(g) for g in range(GMLP_GROUPS)])

    lane_lo = lax.broadcasted_iota(jnp.int32, (1, 128), 1) < SWA_HEAD_DIM
    first = jnp.where(t == 0, -jnp.inf, 0.0)
    yd_chunks = []
    att = {}

    def attn_task(c, j):
        def run():
            r = slice(c * CHUNK, (c + 1) * CHUNK)
            if j == 0:
                kc, vc = z["kn"][r, :], z["vd"][r, :]
                kcat = jnp.concatenate([kprev_ref[...], kc], axis=0)
                vcat = jnp.concatenate([vprev_ref[...], vc], axis=0)
                kprev_ref[...] = kc
                vprev_ref[...] = vc
                kroll = pltpu.roll(kcat, 64, 1)
                vroll = pltpu.roll(vcat, 64, 1)
                att["k"] = [[jnp.where(lane_lo, kcat, 0.0).astype(BF16), jnp.where(lane_lo, 0.0, kroll).astype(BF16)],
                            [jnp.where(lane_lo, kroll, 0.0).astype(BF16), jnp.where(lane_lo, 0.0, kcat).astype(BF16)]]
                att["v"] = [[jnp.where(lane_lo, vcat, 0.0).astype(BF16), jnp.where(lane_lo, 0.0, vroll).astype(BF16)],
                            [jnp.where(lane_lo, vroll, 0.0).astype(BF16), jnp.where(lane_lo, 0.0, vcat).astype(BF16)]]
                att["outs"] = []
            kv = (2 * j) // SWA_GROUP
            qblk = z["qn"][r, j * 128:(j + 1) * 128].astype(BF16)
            ps, rinv = [], []
            for pos in range(2):
                hidx = 2 * j + pos
                s = _mm_nt(qblk, att["k"][kv][pos]) + bias_ref[hidx]
                if c == 0:
                    s = jnp.concatenate([s[:, :CHUNK] + first, s[:, CHUNK:]], axis=1)
                sink = sinks_ref[layer, hidx]
                m = jnp.maximum(jnp.max(s, axis=-1, keepdims=True), sink)
                p = jnp.exp(s - m)
                den = jnp.sum(p, axis=-1, keepdims=True) + jnp.exp(sink - m)
                ps.append(p.astype(BF16))
                rinv.append(1.0 / den)
            o = _mm(jnp.concatenate(ps, axis=1),
                    jnp.concatenate([att["v"][kv][0], att["v"][kv][1]], axis=0))
            att["outs"].append(o * jnp.where(lane_lo, rinv[0], rinv[1]))
            if j == SWA_HEADS // 2 - 1:
                yd_chunks.append(jnp.concatenate(att["outs"], axis=1))
        return run

    _interleave([attn_task(c, j) for c in range(n_chunks) for j in range(SWA_HEADS // 2)],
                branch_tasks(0, lambda: jnp.concatenate(ya_chunks, axis=0))
                + branch_tasks(1, lambda: yb_ref[...])
                + branch_tasks(2, lambda: z["ug"] * jnp.concatenate(s_groups, axis=1)))
    for task in branch_tasks(3, lambda: jnp.concatenate(yd_chunks, axis=0)):
        task()

    o_ref[0] = x + _mm(jnp.concatenate(mix, axis=1).astype(BF16), wout_ref[...])


def _ffn_kernel(x_ref, g_ref, win_ref, wout_ref, o_ref):
    x = x_ref[...]
    hb = _rms_norm(x, g_ref[...]).astype(BF16)
    gh = _mm(hb, win_ref[:, :D_FF])
    up = _mm(hb, win_ref[:, D_FF:])
    o_ref[...] = x + _mm((_silu_of_half(gh) * up).astype(BF16), wout_ref[...])


def _const_spec(shape):
    nd = len(shape)
    return pl.BlockSpec(shape, lambda *_: (0,) * nd, pipeline_mode=pl.Buffered(1))


def _layer_spec(arr, layer):
    nd = arr.ndim - 1
    return pl.BlockSpec((None,) + arr.shape[1:], lambda *_: (layer,) + (0,) * nd, pipeline_mode=pl.Buffered(1))


def _t5_bucket(dist):
    max_exact = REL_BUCKETS // 2
    d = jnp.maximum(dist, 1).astype(F32)
    large = max_exact + (jnp.log(d / max_exact) / math.log(REL_MAX_DIST / max_exact)
                         * (REL_BUCKETS - max_exact)).astype(jnp.int32)
    large = jnp.minimum(large, REL_BUCKETS - 1)
    return jnp.where(dist < max_exact, dist, large)


def _position_tables():
    half = RET_DK // 2
    inv = ROPE_BASE ** (-jnp.arange(half, dtype=F32) / half)
    ang = jnp.arange(SEQ).astype(F32)[:, None] * inv[None, :]
    cos, sin = jnp.cos(ang), jnp.sin(ang)
    cos2 = jnp.concatenate([cos, cos], axis=1)
    sin2 = jnp.concatenate([-sin, sin], axis=1)
    kscale = RET_DK ** -0.5
    gamma = 1.0 - 2.0 ** (-5.0 - jnp.arange(RET_HEADS, dtype=F32))
    log_g = jnp.log(gamma)
    idx = jnp.arange(CHUNK, dtype=F32)
    diff = idx[:, None] - idx[None, :]
    decay = jnp.where(diff >= 0, jnp.exp(log_g[:, None, None] * jnp.maximum(diff, 0.0)), 0.0)
    xi = jnp.repeat(jnp.exp(log_g[None, :] * (idx[:, None] + 1.0)), RET_DK, axis=1)
    zeta = jnp.repeat(jnp.exp(log_g[None, :] * (CHUNK - 1.0 - idx[:, None])), RET_DK, axis=1)
    cdec = jnp.repeat(jnp.exp(log_g * CHUNK), RET_DK)[None, :]
    return cos2, sin2, cos2 * kscale, sin2 * kscale, decay, xi, zeta, cdec


def _bucket_table():
    qi = jnp.arange(CHUNK)[:, None] + CHUNK
    kj = jnp.arange(2 * CHUNK)[None, :]
    dist = qi - kj
    in_win = (dist >= 0) & (dist < SWA_WINDOW)
    return jnp.where(in_win, _t5_bucket(jnp.maximum(dist, 0)), -1).astype(jnp.int32)


def _in_proj_column_scale():
    col = jnp.arange(IN_COLS)
    halved = (((col >= OFF_CONV) & (col < OFF_GMLP))
              | ((col >= OFF_RET + 3 * MIX) & (col < OFF_CONV))
              | (col >= OFF_GATE))
    return jnp.where(halved, 0.5, 1.0).astype(F32)


def _mixer_call(layer, x, tables, bucket, hsum, rel_bias, sinks, layer_params):
    B, S, D = x.shape
    ts = MIXER_TILE
    pos_spec = pl.BlockSpec((ts, 128), lambda b, t: (t, 0))
    smem = pl.BlockSpec(memory_space=pltpu.SMEM)
    n1g, win, gng, cw, cb, clg, clb, glg, glb, ws, bs, qg, kg, wbr, wout = layer_params
    lspec = lambda a: _layer_spec(a, layer)
    in_specs = (
        [pl.BlockSpec((1, ts, D), lambda b, t: (b, t, 0))]
        + [pos_spec] * 4 + [_const_spec(a.shape) for a in tables[4:]]
        + [lspec(a) for a in (n1g, win, gng, cw, cb, clg, clb, glg, glb, ws, bs, qg, kg)]
        + [smem, smem, _const_spec(bucket.shape), _const_spec(hsum.shape), lspec(wbr), lspec(wout)])
    return pl.pallas_call(
        functools.partial(_mixer_kernel, layer),
        grid=(B, S // ts),
        in_specs=in_specs,
        out_specs=pl.BlockSpec((1, ts, D), lambda b, t: (b, t, 0)),
        out_shape=jax.ShapeDtypeStruct(x.shape, x.dtype),
        scratch_shapes=[
            pltpu.VMEM((RET_DK, MIX), F32),
            pltpu.VMEM((CONV_TAIL + ts, MIX), F32),
            pltpu.VMEM((SUBLANES, ts + CONV_TAIL - SUBLANES, MIX), F32),
            pltpu.VMEM((ts, MIX), BF16),
            pltpu.VMEM((ts, N_BRANCH * D_MODEL), F32),
            pltpu.VMEM((CHUNK, 128), F32),
            pltpu.VMEM((CHUNK, 128), F32),
            pltpu.VMEM((SWA_HEADS, CHUNK, 2 * CHUNK), F32),
        ],
        compiler_params=pltpu.CompilerParams(
            dimension_semantics=("arbitrary", "arbitrary"),
            vmem_limit_bytes=VMEM_LIMIT_BYTES),
        name="mixer",
    )(x, *tables, n1g, win, gng, cw, cb, clg, clb, glg, glb, ws, bs, qg, kg,
      sinks, rel_bias, bucket, hsum, wbr, wout)


def _ffn_call(layer, x2, g, win, wout):
    T, D = x2.shape
    ts = FFN_TILE
    return pl.pallas_call(
        _ffn_kernel,
        grid=(T // ts,),
        in_specs=[pl.BlockSpec((ts, D), lambda i: (i, 0)),
                  _layer_spec(g, layer), _layer_spec(win, layer), _layer_spec(wout, layer)],
        out_specs=pl.BlockSpec((ts, D), lambda i: (i, 0)),
        out_shape=jax.ShapeDtypeStruct(x2.shape, x2.dtype),
        compiler_params=pltpu.CompilerParams(
            dimension_semantics=("arbitrary",),
            vmem_limit_bytes=VMEM_LIMIT_BYTES),
        name="ffn",
    )(x2, g, win, wout)


def kernel(x, rel_bias, norm1_g, w_in, ret_gn_g, conv_w, conv_b, conv_ln_g, conv_ln_b, gmlp_ln_g, gmlp_ln_b, gmlp_ws, gmlp_bs, swa_q_g, swa_k_g, swa_sinks, w_branch, w_out, norm2_g, w_ffn_in, w_ffn_out):
    B, S, D = x.shape
    depth = w_in.shape[0]
    tables = _position_tables()
    bucket = _bucket_table()
    head_id = jnp.arange(MIX) // SWA_HEAD_DIM
    hsum = jnp.where(head_id[:, None] == head_id[None, :], 1.0 / SWA_HEAD_DIM, 0.0).astype(BF16)
    rows = lambda v: v.reshape(depth, 1, -1).astype(F32)
    layer_params = (
        rows(norm1_g), (w_in * _in_proj_column_scale()).astype(BF16), rows(ret_gn_g),
        jnp.broadcast_to(conv_w.reshape(depth, CONV_WIDTH, 1, MIX).astype(F32),
                         (depth, CONV_WIDTH, SUBLANES, MIX)),
        rows(conv_b), rows(conv_ln_g), rows(conv_ln_b), rows(gmlp_ln_g), rows(gmlp_ln_b),
        gmlp_ws.astype(F32),
        jnp.broadcast_to(gmlp_bs.astype(F32)[..., None], (depth, GMLP_GROUPS, CHUNK, 128)),
        rows(jnp.tile(swa_q_g, (1, SWA_HEADS))) * (SWA_HEAD_DIM ** -0.5),
        rows(jnp.tile(swa_k_g, (1, SWA_KV_HEADS))),
        w_branch.astype(BF16), (w_out * 0.5).astype(BF16))
    ffn_scale = jnp.where(jnp.arange(2 * D_FF) < D_FF, 0.5, 1.0).astype(F32)
    n2g, wfi, wfo = rows(norm2_g), (w_ffn_in * ffn_scale).astype(BF16), w_ffn_out.astype(BF16)
    for l in range(depth):
        x = _mixer_call(l, x, tables, bucket, hsum, rel_bias.astype(F32), swa_sinks.astype(F32), layer_params)
        x = _ffn_call(l, x.reshape(B * S, D), n2g, wfi, wfo).reshape(B, S, D)
    return x
```

```python
import functools
import math

import jax
import jax.numpy as jnp
from jax import lax
from jax.experimental import pallas as pl
from jax.experimental.pallas import tpu as pltpu

D_MODEL = 1024
SEQ = 8192
EPS = 1e-6
MIX = 512
N_BRANCH = 4
CHUNK = 128
RET_HEADS = 4
RET_DK = 128
ROPE_BASE = 10000.0
CONV_WIDTH = 31
CONV_TAIL = 32
CONV_ROWS = 32
SUBLANES = 8
GMLP_GROUPS = 4
SWA_HEADS = 8
SWA_KV_HEADS = 2
SWA_GROUP = SWA_HEADS // SWA_KV_HEADS
SWA_HEAD_DIM = 64
SWA_WINDOW = 128
REL_BUCKETS = 32
REL_MAX_DIST = 128
D_FF = 2816
OFF_RET = 0
OFF_CONV = 2048
OFF_GMLP = 3072
OFF_SWA = 4096
OFF_GATE = 4864
IN_COLS = 8960

MIXER_TILE = 256
FFN_TILE = 512
COL_BLOCK = 256
VMEM_LIMIT_BYTES = 58 * 1024 * 1024

F32 = jnp.float32
BF16 = jnp.bfloat16


def _mm(a, b):
    return jnp.dot(a, b, preferred_element_type=F32)


def _mm_nt(a, b):
    return lax.dot_general(a, b, (((1,), (1,)), ((), ())), preferred_element_type=F32)


def _sigmoid(x):
    return 0.5 * jnp.tanh(0.5 * x) + 0.5


def _silu(x):
    return x * _sigmoid(x)


def _silu_of_half(h):
    return h * jnp.tanh(h) + h


def _gelu_tanh(x):
    c = math.sqrt(2.0 / math.pi)
    return x * (0.5 * (1.0 + jnp.tanh(c * (x + 0.044715 * (x * x * x)))))


def _rms_norm(x, g):
    return x * lax.rsqrt(jnp.mean(x * x, axis=-1, keepdims=True) + EPS) * g


def _layer_norm(x, g, b):
    mu = jnp.mean(x, axis=-1, keepdims=True)
    d = x - mu
    var = jnp.mean(d * d, axis=-1, keepdims=True)
    return d * lax.rsqrt(var + EPS) * g + b


def _interleave(*streams):
    keyed = []
    for s, stream in enumerate(streams):
        keyed += [((i + 0.5) / len(stream), s, i, fn) for i, fn in enumerate(stream)]
    for _, _, _, fn in sorted(keyed, key=lambda e: e[:3]):
        fn()


def _mixer_kernel(layer, x_ref, rope_ref, decay_ref, xi_ref, zeta_ref, cdec_ref,
                  n1g_ref, win_ref, gng_ref, cw_ref, cb_ref, clg_ref, clb_ref, glg_ref, glb_ref,
                  ws_ref, bs_ref, qg_ref, kg_ref, sinks_ref, rel_ref, bucket_ref, hsum_ref, wbr_ref, wout_ref,
                  o_ref, state_ref, ubuf_ref, ushift_ref, yb_ref, gates_ref, kprev_ref, vprev_ref, bias_ref):
    ts = x_ref.shape[1]
    n_chunks = ts // CHUNK
    t = pl.program_id(1)

    @pl.when((pl.program_id(0) == 0) & (t == 0))
    def _():
        bucket = bucket_ref[...]
        for h in range(SWA_HEADS):
            bias_ref[h] = lax.fori_loop(
                0, REL_BUCKETS, lambda k, acc: jnp.where(bucket == k, rel_ref[k, h], acc),
                jnp.full(bucket.shape, -jnp.inf, F32))

    @pl.when(t == 0)
    def _():
        state_ref[...] = jnp.zeros_like(state_ref)
        ubuf_ref[0:CONV_TAIL, :] = jnp.zeros((CONV_TAIL, MIX), F32)
        kprev_ref[...] = jnp.zeros_like(kprev_ref)
        vprev_ref[...] = jnp.zeros_like(vprev_ref)

    x = x_ref[0]
    hb = _rms_norm(x, n1g_ref[...]).astype(BF16)
    z = {}

    def in_proj(off, width):
        return _mm(hb, win_ref[:, off:off + width])

    def proj_task(name, off, width):
        def run():
            z[name] = in_proj(off, width)
        return run

    def gate_task(i):
        def run():
            c0 = i * COL_BLOCK
            gates_ref[:, c0:c0 + COL_BLOCK] = jnp.tanh(in_proj(OFF_GATE + c0, COL_BLOCK))
        return run

    gate_tasks = [gate_task(i) for i in range(N_BRANCH * D_MODEL // COL_BLOCK)]

    mix = [None] * (D_MODEL // COL_BLOCK)

    def branch_tasks(b, get_y):
        def task(j):
            def run():
                if j == 0:
                    z["ybf", b] = get_y().astype(BF16)
                c0 = j * COL_BLOCK
                br = _mm(z["ybf", b], wbr_ref[b, :, c0:c0 + COL_BLOCK])
                term = gates_ref[:, b * D_MODEL + c0:b * D_MODEL + c0 + COL_BLOCK] * br + br
                mix[j] = term if mix[j] is None else mix[j] + term
            return run
        return [task(j) for j in range(D_MODEL // COL_BLOCK)]

    zb = in_proj(OFF_CONV, 2 * MIX)
    ubuf_ref[CONV_TAIL:CONV_TAIL + ts, :] = zb[:, :MIX] * jnp.tanh(zb[:, MIX:]) + zb[:, :MIX]
    base = CONV_TAIL - (CONV_WIDTH - 1)
    taps = []
    shift_tasks = []
    for b in range(SUBLANES):
        n_a = (CONV_WIDTH - b + SUBLANES - 1) // SUBLANES
        off = base + b
        if off % SUBLANES == 0:
            src, src_off = ubuf_ref, off
        else:
            def shift(b=b, off=off, rows=ts + SUBLANES * (n_a - 1)):
                ushift_ref[b, 0:rows, :] = ubuf_ref[off:off + rows, :]
            shift_tasks.append(shift)
            src, src_off = ushift_ref.at[b], 0
        taps += [(SUBLANES * a + b, src, src_off + SUBLANES * a) for a in range(n_a)]

    _interleave(shift_tasks,
                [proj_task("qk", OFF_RET, 1024), proj_task("vg", OFF_RET + 1024, 1024),
                 proj_task("c", OFF_GMLP, 1024), proj_task("d", OFF_SWA, 768)])

    def gmlp_front():
        zc = _gelu_tanh(z["c"])
        z["ug"] = zc[:, :MIX]
        z["vg16"] = _layer_norm(zc[:, MIX:], glg_ref[...], glb_ref[...]).astype(BF16)

    def swa_front():
        zd = z["d"]
        qd, kd = zd[:, :512], zd[:, 512:640]
        hsum = hsum_ref[...]
        z["qn"] = qd * lax.rsqrt(_mm((qd * qd).astype(BF16), hsum) + EPS) * qg_ref[...]
        z["kn"] = kd * lax.rsqrt(_mm((kd * kd).astype(BF16), hsum[0:128, 0:128]) + EPS) * kg_ref[...]
        z["vd"] = zd[:, 640:768]

    n_gate = len(gate_tasks)
    _interleave([gmlp_front, swa_front], gate_tasks[0:n_gate // 4])

    cb, clg, clb = cb_ref[...], clg_ref[...], clb_ref[...]

    def conv_task(r0):
        def run():
            acc = cb
            for j, src, off in taps:
                w = jnp.concatenate([cw_ref[j]] * (CONV_ROWS // SUBLANES), axis=0)
                acc = acc + w * src[off + r0:off + r0 + CONV_ROWS, :]
            yb_ref[r0:r0 + CONV_ROWS, :] = _silu(_layer_norm(acc, clg, clb)).astype(BF16)
        return run

    _interleave([conv_task(r0) for r0 in range(0, ts, CONV_ROWS)], gate_tasks[n_gate // 4:3 * n_gate // 4])
    ubuf_ref[0:CONV_TAIL, :] = ubuf_ref[ts:ts + CONV_TAIL, :]

    gng = gng_ref[...]
    cdec = cdec_ref[...]
    ya_chunks = []
    ret = {}

    def ret_task(c, hd):
        def run():
            r = slice(c * CHUNK, (c + 1) * CHUNK)
            l = slice(hd * RET_DK, (hd + 1) * RET_DK)
            if hd == 0:
                ret["st"] = state_ref[...]
                ret["heads"], ret["new_state"] = [], []
            st = ret["st"]
            q = z["qk"][r, hd * 128:(hd + 1) * 128]
            k = z["qk"][r, 512 + hd * 128:512 + (hd + 1) * 128]
            v = z["vg"][r, hd * 128:(hd + 1) * 128]
            q = q * rope_ref[r, 0:128] + pltpu.roll(q, 64, 1) * rope_ref[r, 128:256]
            k = k * rope_ref[r, 256:384] + pltpu.roll(k, 64, 1) * rope_ref[r, 384:512]
            vb = v.astype(BF16)
            s = _mm_nt(q.astype(BF16), k.astype(BF16)) * decay_ref[hd]
            lhs = jnp.concatenate([s.astype(BF16), (q * xi_ref[:, l]).astype(BF16)], axis=1)
            rhs = jnp.concatenate([vb, st[:, l].astype(BF16)], axis=0)
            y = _mm(lhs, rhs)
            kzt = (k * zeta_ref[:, l]).T.astype(BF16)
            ret["new_state"].append(cdec[:, l] * st[:, l] + _mm(kzt, vb))
            mu = jnp.mean(y, axis=-1, keepdims=True)
            d = y - mu
            var = jnp.mean(d * d, axis=-1, keepdims=True)
            ret["heads"].append(d * lax.rsqrt(var + EPS))
            if hd == RET_HEADS - 1:
                state_ref[...] = jnp.concatenate(ret["new_state"], axis=1)
                yn = jnp.concatenate(ret["heads"], axis=1) * gng
                ya_chunks.append(_silu_of_half(z["vg"][r, 512:1024]) * yn)
        return run

    tril = (lax.broadcasted_iota(jnp.int32, (CHUNK, CHUNK), 0)
            >= lax.broadcasted_iota(jnp.int32, (CHUNK, CHUNK), 1))
    s_groups = []

    def gmlp_task(g):
        def run():
            wsm = jnp.where(tril, ws_ref[g], 0.0).astype(BF16)
            vg = z["vg16"]
            rhs = jnp.concatenate([vg[c * CHUNK:(c + 1) * CHUNK, g * 128:(g + 1) * 128]
                                   for c in range(n_chunks)], axis=1)
            sg = _mm(wsm, rhs)
            bsg = bs_ref[g]
            s_groups.append(jnp.concatenate([sg[:, c * 128:(c + 1) * 128] + bsg
                                             for c in range(n_chunks)], axis=0))
        return run

    lane_lo = lax.broadcasted_iota(jnp.int32, (1, 128), 1) < SWA_HEAD_DIM
    first = jnp.where(t == 0, -jnp.inf, 0.0)
    yd_chunks = []
    att = {}

    def attn_task(c, j):
        def run():
            r = slice(c * CHUNK, (c + 1) * CHUNK)
            if j == 0:
                kc, vc = z["kn"][r, :], z["vd"][r, :]
                kcat = jnp.concatenate([kprev_ref[...], kc], axis=0)
                vcat = jnp.concatenate([vprev_ref[...], vc], axis=0)
                kprev_ref[...] = kc
                vprev_ref[...] = vc
                kroll = pltpu.roll(kcat, 64, 1)
                vroll = pltpu.roll(vcat, 64, 1)
                att["k"] = [[jnp.where(lane_lo, kcat, 0.0).astype(BF16), jnp.where(lane_lo, 0.0, kroll).astype(BF16)],
                            [jnp.where(lane_lo, kroll, 0.0).astype(BF16), jnp.where(lane_lo, 0.0, kcat).astype(BF16)]]
                att["v"] = [[jnp.where(lane_lo, vcat, 0.0).astype(BF16), jnp.where(lane_lo, 0.0, vroll).astype(BF16)],
                            [jnp.where(lane_lo, vroll, 0.0).astype(BF16), jnp.where(lane_lo, 0.0, vcat).astype(BF16)]]
                att["outs"] = []
            kv = (2 * j) // SWA_GROUP
            qblk = z["qn"][r, j * 128:(j + 1) * 128].astype(BF16)
            ps, rinv = [], []
            for pos in range(2):
                hidx = 2 * j + pos
                s = _mm_nt(qblk, att["k"][kv][pos]) + bias_ref[hidx]
                if c == 0:
                    s = jnp.concatenate([s[:, :CHUNK] + first, s[:, CHUNK:]], axis=1)
                sink = sinks_ref[layer, hidx]
                m = jnp.maximum(jnp.max(s, axis=-1, keepdims=True), sink)
                p = jnp.exp(s - m)
                den = jnp.sum(p, axis=-1, keepdims=True) + jnp.exp(sink - m)
                ps.append(p.astype(BF16))
                rinv.append(1.0 / den)
            o = _mm(jnp.concatenate(ps, axis=1),
                    jnp.concatenate([att["v"][kv][0], att["v"][kv][1]], axis=0))
            att["outs"].append(o * jnp.where(lane_lo, rinv[0], rinv[1]))
            if j == SWA_HEADS // 2 - 1:
                yd_chunks.append(jnp.concatenate(att["outs"], axis=1))
        return run

    _interleave([ret_task(c, hd) for c in range(n_chunks) for hd in range(RET_HEADS)],
                gate_tasks[3 * n_gate // 4:] + [gmlp_task(g) for g in range(GMLP_GROUPS)])
    _interleave([attn_task(c, j) for c in range(n_chunks) for j in range(SWA_HEADS // 2)],
                branch_tasks(0, lambda: jnp.concatenate(ya_chunks, axis=0))
                + branch_tasks(1, lambda: yb_ref[...])
                + branch_tasks(2, lambda: z["ug"] * jnp.concatenate(s_groups, axis=1)))
    for task in branch_tasks(3, lambda: jnp.concatenate(yd_chunks, axis=0)):
        task()

    o_ref[0] = x + _mm(jnp.concatenate(mix, axis=1).astype(BF16), wout_ref[...])


def _ffn_kernel(x_ref, g_ref, win_ref, wout_ref, o_ref):
    x = x_ref[...]
    hb = _rms_norm(x, g_ref[...]).astype(BF16)
    gh = _mm(hb, win_ref[:, :D_FF])
    up = _mm(hb, win_ref[:, D_FF:])
    o_ref[...] = x + _mm((_silu_of_half(gh) * up).astype(BF16), wout_ref[...])


def _const_spec(shape):
    nd = len(shape)
    return pl.BlockSpec(shape, lambda *_: (0,) * nd, pipeline_mode=pl.Buffered(1))


def _layer_spec(arr, layer):
    nd = arr.ndim - 1
    return pl.BlockSpec((None,) + arr.shape[1:], lambda *_: (layer,) + (0,) * nd, pipeline_mode=pl.Buffered(1))


def _t5_bucket(dist):
    max_exact = REL_BUCKETS // 2
    d = jnp.maximum(dist, 1).astype(F32)
    large = max_exact + (jnp.log(d / max_exact) / math.log(REL_MAX_DIST / max_exact)
                         * (REL_BUCKETS - max_exact)).astype(jnp.int32)
    large = jnp.minimum(large, REL_BUCKETS - 1)
    return jnp.where(dist < max_exact, dist, large)


def _position_tables():
    half = RET_DK // 2
    inv = ROPE_BASE ** (-jnp.arange(half, dtype=F32) / half)
    ang = jnp.arange(SEQ).astype(F32)[:, None] * inv[None, :]
    cos, sin = jnp.cos(ang), jnp.sin(ang)
    cos2 = jnp.concatenate([cos, cos], axis=1)
    sin2 = jnp.concatenate([-sin, sin], axis=1)
    kscale = RET_DK ** -0.5
    gamma = 1.0 - 2.0 ** (-5.0 - jnp.arange(RET_HEADS, dtype=F32))
    log_g = jnp.log(gamma)
    idx = jnp.arange(CHUNK, dtype=F32)
    diff = idx[:, None] - idx[None, :]
    decay = jnp.where(diff >= 0, jnp.exp(log_g[:, None, None] * jnp.maximum(diff, 0.0)), 0.0)
    xi = jnp.repeat(jnp.exp(log_g[None, :] * (idx[:, None] + 1.0)), RET_DK, axis=1)
    zeta = jnp.repeat(jnp.exp(log_g[None, :] * (CHUNK - 1.0 - idx[:, None])), RET_DK, axis=1)
    cdec = jnp.repeat(jnp.exp(log_g * CHUNK), RET_DK)[None, :]
    rope = jnp.concatenate([cos2, sin2, cos2 * kscale, sin2 * kscale], axis=1)
    return rope, decay, xi, zeta, cdec


def _bucket_table():
    qi = jnp.arange(CHUNK)[:, None] + CHUNK
    kj = jnp.arange(2 * CHUNK)[None, :]
    dist = qi - kj
    in_win = (dist >= 0) & (dist < SWA_WINDOW)
    return jnp.where(in_win, _t5_bucket(jnp.maximum(dist, 0)), -1).astype(jnp.int32)


def _in_proj_column_scale():
    col = jnp.arange(IN_COLS)
    halved = (((col >= OFF_CONV) & (col < OFF_GMLP))
              | ((col >= OFF_RET + 3 * MIX) & (col < OFF_CONV))
              | (col >= OFF_GATE))
    return jnp.where(halved, 0.5, 1.0).astype(F32)


def _mixer_call(layer, x, tables, bucket, hsum, rel_bias, sinks, layer_params):
    B, S, D = x.shape
    ts = MIXER_TILE
    pos_spec = pl.BlockSpec((ts, tables[0].shape[1]), lambda b, t: (t, 0))
    smem = pl.BlockSpec(memory_space=pltpu.SMEM)
    n1g, win, gng, cw, cb, clg, clb, glg, glb, ws, bs, qg, kg, wbr, wout = layer_params
    lspec = lambda a: _layer_spec(a, layer)
    in_specs = (
        [pl.BlockSpec((1, ts, D), lambda b, t: (b, t, 0))]
        + [pos_spec] + [_const_spec(a.shape) for a in tables[1:]]
        + [lspec(a) for a in (n1g, win, gng, cw, cb, clg, clb, glg, glb, ws, bs, qg, kg)]
        + [smem, smem, _const_spec(bucket.shape), _const_spec(hsum.shape), lspec(wbr), lspec(wout)])
    return pl.pallas_call(
        functools.partial(_mixer_kernel, layer),
        grid=(B, S // ts),
        in_specs=in_specs,
        out_specs=pl.BlockSpec((1, ts, D), lambda b, t: (b, t, 0)),
        out_shape=jax.ShapeDtypeStruct(x.shape, x.dtype),
        scratch_shapes=[
            pltpu.VMEM((RET_DK, MIX), F32),
            pltpu.VMEM((CONV_TAIL + ts, MIX), F32),
            pltpu.VMEM((SUBLANES, ts + CONV_TAIL - SUBLANES, MIX), F32),
            pltpu.VMEM((ts, MIX), BF16),
            pltpu.VMEM((ts, N_BRANCH * D_MODEL), F32),
            pltpu.VMEM((CHUNK, 128), F32),
            pltpu.VMEM((CHUNK, 128), F32),
            pltpu.VMEM((SWA_HEADS, CHUNK, 2 * CHUNK), F32),
        ],
        compiler_params=pltpu.CompilerParams(
            dimension_semantics=("arbitrary", "arbitrary"),
            vmem_limit_bytes=VMEM_LIMIT_BYTES),
        name="mixer",
    )(x, *tables, n1g, win, gng, cw, cb, clg, clb, glg, glb, ws, bs, qg, kg,
      sinks, rel_bias, bucket, hsum, wbr, wout)


def _ffn_call(layer, x2, g, win, wout):
    T, D = x2.shape
    ts = FFN_TILE
    return pl.pallas_call(
        _ffn_kernel,
        grid=(T // ts,),
        in_specs=[pl.BlockSpec((ts, D), lambda i: (i, 0)),
                  _layer_spec(g, layer), _layer_spec(win, layer), _layer_spec(wout, layer)],
        out_specs=pl.BlockSpec((ts, D), lambda i: (i, 0)),
        out_shape=jax.ShapeDtypeStruct(x2.shape, x2.dtype),
        compiler_params=pltpu.CompilerParams(
            dimension_semantics=("arbitrary",),
            vmem_limit_bytes=VMEM_LIMIT_BYTES),
        name="ffn",
    )(x2, g, win, wout)


def kernel(x, rel_bias, norm1_g, w_in, ret_gn_g, conv_w, conv_b, conv_ln_g, conv_ln_b, gmlp_ln_g, gmlp_ln_b, gmlp_ws, gmlp_bs, swa_q_g, swa_k_g, swa_sinks, w_branch, w_out, norm2_g, w_ffn_in, w_ffn_out):
    B, S, D = x.shape
    depth = w_in.shape[0]
    tables = _position_tables()
    bucket = _bucket_table()
    head_id = jnp.arange(MIX) // SWA_HEAD_DIM
    hsum = jnp.where(head_id[:, None] == head_id[None, :], 1.0 / SWA_HEAD_DIM, 0.0).astype(BF16)
    rows = lambda v: v.reshape(depth, 1, -1).astype(F32)
    layer_params = (
        rows(norm1_g), (w_in * _in_proj_column_scale()).astype(BF16), rows(ret_gn_g),
        jnp.broadcast_to(conv_w.reshape(depth, CONV_WIDTH, 1, MIX).astype(F32),
                         (depth, CONV_WIDTH, SUBLANES, MIX)),
        rows(conv_b), rows(conv_ln_g), rows(conv_ln_b), rows(gmlp_ln_g), rows(gmlp_ln_b),
        gmlp_ws.astype(F32),
        jnp.broadcast_to(gmlp_bs.astype(F32)[..., None], (depth, GMLP_GROUPS, CHUNK, 128)),
        rows(jnp.tile(swa_q_g, (1, SWA_HEADS))) * (SWA_HEAD_DIM ** -0.5),
        rows(jnp.tile(swa_k_g, (1, SWA_KV_HEADS))),
        w_branch.astype(BF16), (w_out * 0.5).astype(BF16))
    ffn_scale = jnp.where(jnp.arange(2 * D_FF) < D_FF, 0.5, 1.0).astype(F32)
    n2g, wfi, wfo = rows(norm2_g), (w_ffn_in * ffn_scale).astype(BF16), w_ffn_out.astype(BF16)
    for l in range(depth):
        x = _mixer_call(l, x, tables, bucket, hsum, rel_bias.astype(F32), swa_sinks.astype(F32), layer_params)
        x = _ffn_call(l, x.reshape(B * S, D), n2g, wfi, wfo).reshape(B, S, D)
    return x
```

```python
import functools
import math

import jax
import jax.numpy as jnp
from jax import lax
from jax.experimental import pallas as pl
from jax.experimental.pallas import tpu as pltpu

D_MODEL = 1024
SEQ = 8192
EPS = 1e-6
MIX = 512
N_BRANCH = 4
CHUNK = 128
RET_HEADS = 4
RET_DK = 128
ROPE_BASE = 10000.0
CONV_WIDTH = 31
CONV_TAIL = 32
CONV_ROWS = 32
SUBLANES = 8
GMLP_GROUPS = 4
SWA_HEADS = 8
SWA_KV_HEADS = 2
SWA_GROUP = SWA_HEADS // SWA_KV_HEADS
SWA_HEAD_DIM = 64
SWA_WINDOW = 128
REL_BUCKETS = 32
REL_MAX_DIST = 128
D_FF = 2816
OFF_RET = 0
OFF_CONV = 2048
OFF_GMLP = 3072
OFF_SWA = 4096
OFF_GATE = 4864
IN_COLS = 8960

MIXER_TILE = 256
FFN_TILE = 512
COL_BLOCK = 256
VMEM_LIMIT_BYTES = 58 * 1024 * 1024

F32 = jnp.float32
BF16 = jnp.bfloat16


def _mm(a, b):
    return jnp.dot(a, b, preferred_element_type=F32)


def _sigmoid(x):
    return 0.5 * jnp.tanh(0.5 * x) + 0.5


def _silu(x):
    return x * _sigmoid(x)


def _silu_of_half(h):
    return h * jnp.tanh(h) + h


def _gelu_tanh(x):
    c = math.sqrt(2.0 / math.pi)
    return x * (0.5 * (1.0 + jnp.tanh(c * (x + 0.044715 * (x * x * x)))))


def _rms_norm(x, g):
    return x * lax.rsqrt(jnp.mean(x * x, axis=-1, keepdims=True) + EPS) * g


def _layer_norm(x, g, b):
    mu = jnp.mean(x, axis=-1, keepdims=True)
    d = x - mu
    var = jnp.mean(d * d, axis=-1, keepdims=True)
    return d * lax.rsqrt(var + EPS) * g + b


def _interleave(*streams):
    keyed = []
    for s, stream in enumerate(streams):
        keyed += [((i + 0.5) / len(stream), s, i, fn) for i, fn in enumerate(stream)]
    for _, _, _, fn in sorted(keyed, key=lambda e: e[:3]):
        fn()


def _mixer_kernel(layer, x_ref, cq_ref, sq_ref, ck_ref, sk_ref, decay_ref, xi_ref, zetat_ref, cdec_ref,
                  n1g_ref, win_ref, gng_ref, cw_ref, cb_ref, clg_ref, clb_ref, glg_ref, glb_ref,
                  ws_ref, bs_ref, qg_ref, kg_ref, sinks_ref, rel_ref, bucket_ref, hsum_ref, wbr_ref, wout_ref,
                  o_ref, state_ref, ubuf_ref, ushift_ref, yb_ref, gates_ref, kprev_ref, vprev_ref, bias_ref):
    ts = x_ref.shape[1]
    n_chunks = ts // CHUNK
    t = pl.program_id(1)

    @pl.when((pl.program_id(0) == 0) & (t == 0))
    def _():
        bucket = bucket_ref[...]
        for h in range(SWA_HEADS):
            bias_ref[h] = lax.fori_loop(
                0, REL_BUCKETS, lambda k, acc: jnp.where(bucket == k, rel_ref[k, h], acc),
                jnp.full(bucket.shape, -jnp.inf, F32))

    @pl.when(t == 0)
    def _():
        state_ref[...] = jnp.zeros_like(state_ref)
        ubuf_ref[0:CONV_TAIL, :] = jnp.zeros((CONV_TAIL, MIX), F32)
        kprev_ref[...] = jnp.zeros_like(kprev_ref)
        vprev_ref[...] = jnp.zeros_like(vprev_ref)

    x = x_ref[0]
    hb = _rms_norm(x, n1g_ref[...]).astype(BF16)
    z = {}

    def in_proj(off, width):
        return _mm(hb, win_ref[:, off:off + width])

    def proj_task(name, off, width):
        def run():
            z[name] = in_proj(off, width)
        return run

    def gate_task(i):
        def run():
            c0 = i * COL_BLOCK
            gates_ref[:, c0:c0 + COL_BLOCK] = jnp.tanh(in_proj(OFF_GATE + c0, COL_BLOCK))
        return run

    gate_tasks = [gate_task(i) for i in range(N_BRANCH * D_MODEL // COL_BLOCK)]

    mix = [None] * (D_MODEL // COL_BLOCK)

    def branch_tasks(b, get_y):
        def task(j):
            def run():
                if j == 0:
                    z["ybf", b] = get_y().astype(BF16)
                c0 = j * COL_BLOCK
                br = _mm(z["ybf", b], wbr_ref[b, :, c0:c0 + COL_BLOCK])
                term = gates_ref[:, b * D_MODEL + c0:b * D_MODEL + c0 + COL_BLOCK] * br + br
                mix[j] = term if mix[j] is None else mix[j] + term
            return run
        return [task(j) for j in range(D_MODEL // COL_BLOCK)]

    zb = in_proj(OFF_CONV, 2 * MIX)
    ubuf_ref[CONV_TAIL:CONV_TAIL + ts, :] = zb[:, :MIX] * jnp.tanh(zb[:, MIX:]) + zb[:, :MIX]
    base = CONV_TAIL - (CONV_WIDTH - 1)
    taps = []
    shift_tasks = []
    for b in range(SUBLANES):
        n_a = (CONV_WIDTH - b + SUBLANES - 1) // SUBLANES
        off = base + b
        if off % SUBLANES == 0:
            src, src_off = ubuf_ref, off
        else:
            def shift(b=b, off=off, rows=ts + SUBLANES * (n_a - 1)):
                ushift_ref[b, 0:rows, :] = ubuf_ref[off:off + rows, :]
            shift_tasks.append(shift)
            src, src_off = ushift_ref.at[b], 0
        taps += [(SUBLANES * a + b, src, src_off + SUBLANES * a) for a in range(n_a)]

    _interleave(shift_tasks,
                [proj_task("qk", OFF_RET, 1024), proj_task("vg", OFF_RET + 1024, 1024),
                 proj_task("c", OFF_GMLP, 1024), proj_task("d", OFF_SWA, 768)])

    def gmlp_front():
        zc = _gelu_tanh(z["c"])
        z["ug"] = zc[:, :MIX]
        z["vg16"] = _layer_norm(zc[:, MIX:], glg_ref[...], glb_ref[...]).astype(BF16)

    def swa_front():
        zd = z["d"]
        qd, kd = zd[:, :512], zd[:, 512:640]
        hsum = hsum_ref[...]
        z["qn"] = qd * lax.rsqrt(_mm((qd * qd).astype(BF16), hsum) + EPS) * qg_ref[...]
        z["kn"] = kd * lax.rsqrt(_mm((kd * kd).astype(BF16), hsum[0:128, 0:128]) + EPS) * kg_ref[...]
        z["vd"] = zd[:, 640:768]

    n_gate = len(gate_tasks)
    _interleave([gmlp_front, swa_front], gate_tasks[0:n_gate // 4])

    cb, clg, clb = cb_ref[...], clg_ref[...], clb_ref[...]

    def conv_task(r0):
        def run():
            acc = cb
            for j, src, off in taps:
                w = jnp.concatenate([cw_ref[j]] * (CONV_ROWS // SUBLANES), axis=0)
                acc = acc + w * src[off + r0:off + r0 + CONV_ROWS, :]
            yb_ref[r0:r0 + CONV_ROWS, :] = _silu(_layer_norm(acc, clg, clb)).astype(BF16)
        return run

    _interleave([conv_task(r0) for r0 in range(0, ts, CONV_ROWS)], gate_tasks[n_gate // 4:3 * n_gate // 4])
    ubuf_ref[0:CONV_TAIL, :] = ubuf_ref[ts:ts + CONV_TAIL, :]

    gng = gng_ref[...]
    cdec = cdec_ref[...]
    ya_chunks = []
    ret = {}

    def ret_task(c, hd):
        def run():
            r = slice(c * CHUNK, (c + 1) * CHUNK)
            l = slice(hd * RET_DK, (hd + 1) * RET_DK)
            if hd == 0:
                ret["st"] = state_ref[...]
                ret["heads"], ret["new_state"] = [], []
            st = ret["st"]
            q = z["qk"][r, hd * 128:(hd + 1) * 128]
            k = z["qk"][r, 512 + hd * 128:512 + (hd + 1) * 128]
            v = z["vg"][r, hd * 128:(hd + 1) * 128]
            q = q * cq_ref[r, :] + pltpu.roll(q, 64, 1) * sq_ref[r, :]
            k = k * ck_ref[r, :] + pltpu.roll(k, 64, 1) * sk_ref[r, :]
            vb = v.astype(BF16)
            kt = k.T
            s = _mm(q.astype(BF16), kt.astype(BF16)) * decay_ref[hd]
            lhs = jnp.concatenate([s.astype(BF16), (q * xi_ref[:, l]).astype(BF16)], axis=1)
            rhs = jnp.concatenate([vb, st[:, l].astype(BF16)], axis=0)
            y = _mm(lhs, rhs)
            kzt = (kt * zetat_ref[hd]).astype(BF16)
            ret["new_state"].append(cdec[:, l] * st[:, l] + _mm(kzt, vb))
            mu = jnp.mean(y, axis=-1, keepdims=True)
            d = y - mu
            var = jnp.mean(d * d, axis=-1, keepdims=True)
            ret["heads"].append(d * lax.rsqrt(var + EPS))
            if hd == RET_HEADS - 1:
                state_ref[...] = jnp.concatenate(ret["new_state"], axis=1)
                yn = jnp.concatenate(ret["heads"], axis=1) * gng
                ya_chunks.append(_silu_of_half(z["vg"][r, 512:1024]) * yn)
        return run

    tril = (lax.broadcasted_iota(jnp.int32, (CHUNK, CHUNK), 0)
            >= lax.broadcasted_iota(jnp.int32, (CHUNK, CHUNK), 1))
    s_groups = []

    def gmlp_task(g):
        def run():
            wsm = jnp.where(tril, ws_ref[g], 0.0).astype(BF16)
            vg = z["vg16"]
            rhs = jnp.concatenate([vg[c * CHUNK:(c + 1) * CHUNK, g * 128:(g + 1) * 128]
                                   for c in range(n_chunks)], axis=1)
            sg = _mm(wsm, rhs)
            bsg = bs_ref[g]
            s_groups.append(jnp.concatenate([sg[:, c * 128:(c + 1) * 128] + bsg
                                             for c in range(n_chunks)], axis=0))
        return run

    lane_lo = lax.broadcasted_iota(jnp.int32, (1, 128), 1) < SWA_HEAD_DIM
    first = jnp.where(t == 0, -jnp.inf, 0.0)
    yd_chunks = []
    att = {}

    def attn_task(c, j):
        def run():
            r = slice(c * CHUNK, (c + 1) * CHUNK)
            if j == 0:
                kc, vc = z["kn"][r, :], z["vd"][r, :]
                kcat = jnp.concatenate([kprev_ref[...], kc], axis=0)
                vcat = jnp.concatenate([vprev_ref[...], vc], axis=0)
                kprev_ref[...] = kc
                vprev_ref[...] = vc
                vroll = pltpu.roll(vcat, 64, 1)
                kt = kcat.T
                kt0, kt1 = kt[0:SWA_HEAD_DIM, :], kt[SWA_HEAD_DIM:2 * SWA_HEAD_DIM, :]
                zero = jnp.zeros_like(kt0)
                att["k"] = [[jnp.concatenate([kt0, zero], axis=0).astype(BF16),
                             jnp.concatenate([zero, kt0], axis=0).astype(BF16)],
                            [jnp.concatenate([kt1, zero], axis=0).astype(BF16),
                             jnp.concatenate([zero, kt1], axis=0).astype(BF16)]]
                att["v"] = [[jnp.where(lane_lo, vcat, 0.0).astype(BF16), jnp.where(lane_lo, 0.0, vroll).astype(BF16)],
                            [jnp.where(lane_lo, vroll, 0.0).astype(BF16), jnp.where(lane_lo, 0.0, vcat).astype(BF16)]]
                att["outs"] = []
            kv = (2 * j) // SWA_GROUP
            qblk = z["qn"][r, j * 128:(j + 1) * 128].astype(BF16)
            ps, rinv = [], []
            for pos in range(2):
                hidx = 2 * j + pos
                s = _mm(qblk, att["k"][kv][pos]) + bias_ref[hidx]
                if c == 0:
                    s = jnp.concatenate([s[:, :CHUNK] + first, s[:, CHUNK:]], axis=1)
                sink = sinks_ref[layer, hidx]
                m = jnp.maximum(jnp.max(s, axis=-1, keepdims=True), sink)
                p = jnp.exp(s - m)
                den = jnp.sum(p, axis=-1, keepdims=True) + jnp.exp(sink - m)
                ps.append(p.astype(BF16))
                rinv.append(1.0 / den)
            o = _mm(jnp.concatenate(ps, axis=1),
                    jnp.concatenate([att["v"][kv][0], att["v"][kv][1]], axis=0))
            att["outs"].append(o * jnp.where(lane_lo, rinv[0], rinv[1]))
            if j == SWA_HEADS // 2 - 1:
                yd_chunks.append(jnp.concatenate(att["outs"], axis=1))
        return run

    _interleave([ret_task(c, hd) for c in range(n_chunks) for hd in range(RET_HEADS)],
                gate_tasks[3 * n_gate // 4:] + [gmlp_task(g) for g in range(GMLP_GROUPS)])
    _interleave([attn_task(c, j) for c in range(n_chunks) for j in range(SWA_HEADS // 2)],
                branch_tasks(0, lambda: jnp.concatenate(ya_chunks, axis=0))
                + branch_tasks(1, lambda: yb_ref[...])
                + branch_tasks(2, lambda: z["ug"] * jnp.concatenate(s_groups, axis=1)))
    for task in branch_tasks(3, lambda: jnp.concatenate(yd_chunks, axis=0)):
        task()

    o_ref[0] = x + _mm(jnp.concatenate(mix, axis=1).astype(BF16), wout_ref[...])


def _ffn_kernel(x_ref, g_ref, win_ref, wout_ref, o_ref):
    x = x_ref[...]
    hb = _rms_norm(x, g_ref[...]).astype(BF16)
    gh = _mm(hb, win_ref[:, :D_FF])
    up = _mm(hb, win_ref[:, D_FF:])
    o_ref[...] = x + _mm((_silu_of_half(gh) * up).astype(BF16), wout_ref[...])


def _const_spec(shape):
    nd = len(shape)
    return pl.BlockSpec(shape, lambda *_: (0,) * nd, pipeline_mode=pl.Buffered(1))


def _layer_spec(arr, layer):
    nd = arr.ndim - 1
    return pl.BlockSpec((None,) + arr.shape[1:], lambda *_: (layer,) + (0,) * nd, pipeline_mode=pl.Buffered(1))


def _t5_bucket(dist):
    max_exact = REL_BUCKETS // 2
    d = jnp.maximum(dist, 1).astype(F32)
    large = max_exact + (jnp.log(d / max_exact) / math.log(REL_MAX_DIST / max_exact)
                         * (REL_BUCKETS - max_exact)).astype(jnp.int32)
    large = jnp.minimum(large, REL_BUCKETS - 1)
    return jnp.where(dist < max_exact, dist, large)


def _position_tables():
    half = RET_DK // 2
    inv = ROPE_BASE ** (-jnp.arange(half, dtype=F32) / half)
    ang = jnp.arange(SEQ).astype(F32)[:, None] * inv[None, :]
    cos, sin = jnp.cos(ang), jnp.sin(ang)
    cos2 = jnp.concatenate([cos, cos], axis=1)
    sin2 = jnp.concatenate([-sin, sin], axis=1)
    kscale = RET_DK ** -0.5
    gamma = 1.0 - 2.0 ** (-5.0 - jnp.arange(RET_HEADS, dtype=F32))
    log_g = jnp.log(gamma)
    idx = jnp.arange(CHUNK, dtype=F32)
    diff = idx[:, None] - idx[None, :]
    decay = jnp.where(diff >= 0, jnp.exp(log_g[:, None, None] * jnp.maximum(diff, 0.0)), 0.0)
    xi = jnp.repeat(jnp.exp(log_g[None, :] * (idx[:, None] + 1.0)), RET_DK, axis=1)
    zeta = jnp.exp(log_g[None, :] * (CHUNK - 1.0 - idx[:, None]))
    zetat = jnp.broadcast_to(zeta.T[:, None, :], (RET_HEADS, RET_DK, CHUNK))
    cdec = jnp.repeat(jnp.exp(log_g * CHUNK), RET_DK)[None, :]
    return cos2, sin2, cos2 * kscale, sin2 * kscale, decay, xi, zetat, cdec


def _bucket_table():
    qi = jnp.arange(CHUNK)[:, None] + CHUNK
    kj = jnp.arange(2 * CHUNK)[None, :]
    dist = qi - kj
    in_win = (dist >= 0) & (dist < SWA_WINDOW)
    return jnp.where(in_win, _t5_bucket(jnp.maximum(dist, 0)), -1).astype(jnp.int32)


def _in_proj_column_scale():
    col = jnp.arange(IN_COLS)
    halved = (((col >= OFF_CONV) & (col < OFF_GMLP))
              | ((col >= OFF_RET + 3 * MIX) & (col < OFF_CONV))
              | (col >= OFF_GATE))
    return jnp.where(halved, 0.5, 1.0).astype(F32)


def _mixer_call(layer, x, tables, bucket, hsum, rel_bias, sinks, layer_params):
    B, S, D = x.shape
    ts = MIXER_TILE
    pos_spec = pl.BlockSpec((ts, 128), lambda b, t: (t, 0))
    smem = pl.BlockSpec(memory_space=pltpu.SMEM)
    n1g, win, gng, cw, cb, clg, clb, glg, glb, ws, bs, qg, kg, wbr, wout = layer_params
    lspec = lambda a: _layer_spec(a, layer)
    in_specs = (
        [pl.BlockSpec((1, ts, D), lambda b, t: (b, t, 0))]
        + [pos_spec] * 4 + [_const_spec(a.shape) for a in tables[4:]]
        + [lspec(a) for a in (n1g, win, gng, cw, cb, clg, clb, glg, glb, ws, bs, qg, kg)]
        + [smem, smem, _const_spec(bucket.shape), _const_spec(hsum.shape), lspec(wbr), lspec(wout)])
    return pl.pallas_call(
        functools.partial(_mixer_kernel, layer),
        grid=(B, S // ts),
        in_specs=in_specs,
        out_specs=pl.BlockSpec((1, ts, D), lambda b, t: (b, t, 0)),
        out_shape=jax.ShapeDtypeStruct(x.shape, x.dtype),
        scratch_shapes=[
            pltpu.VMEM((RET_DK, MIX), F32),
            pltpu.VMEM((CONV_TAIL + ts, MIX), F32),
            pltpu.VMEM((SUBLANES, ts + CONV_TAIL - SUBLANES, MIX), F32),
            pltpu.VMEM((ts, MIX), BF16),
            pltpu.VMEM((ts, N_BRANCH * D_MODEL), F32),
            pltpu.VMEM((CHUNK, 128), F32),
            pltpu.VMEM((CHUNK, 128), F32),
            pltpu.VMEM((SWA_HEADS, CHUNK, 2 * CHUNK), F32),
        ],
        compiler_params=pltpu.CompilerParams(
            dimension_semantics=("arbitrary", "arbitrary"),
            vmem_limit_bytes=VMEM_LIMIT_BYTES),
        name="mixer",
    )(x, *tables, n1g, win, gng, cw, cb, clg, clb, glg, glb, ws, bs, qg, kg,
      sinks, rel_bias, bucket, hsum, wbr, wout)


def _ffn_call(layer, x2, g, win, wout):
    T, D = x2.shape
    ts = FFN_TILE
    return pl.pallas_call(
        _ffn_kernel,
        grid=(T // ts,),
        in_specs=[pl.BlockSpec((ts, D), lambda i: (i, 0)),
                  _layer_spec(g, layer), _layer_spec(win, layer), _layer_spec(wout, layer)],
        out_specs=pl.BlockSpec((ts, D), lambda i: (i, 0)),
        out_shape=jax.ShapeDtypeStruct(x2.shape, x2.dtype),
        compiler_params=pltpu.CompilerParams(
            dimension_semantics=("arbitrary",),
            vmem_limit_bytes=VMEM_LIMIT_BYTES),
        name="ffn",
    )(x2, g, win, wout)


def kernel(x, rel_bias, norm1_g, w_in, ret_gn_g, conv_w, conv_b, conv_ln_g, conv_ln_b, gmlp_ln_g, gmlp_ln_b, gmlp_ws, gmlp_bs, swa_q_g, swa_k_g, swa_sinks, w_branch, w_out, norm2_g, w_ffn_in, w_ffn_out):
    B, S, D = x.shape
    depth = w_in.shape[0]
    tables = _position_tables()
    bucket = _bucket_table()
    head_id = jnp.arange(MIX) // SWA_HEAD_DIM
    hsum = jnp.where(head_id[:, None] == head_id[None, :], 1.0 / SWA_HEAD_DIM, 0.0).astype(BF16)
    rows = lambda v: v.reshape(depth, 1, -1).astype(F32)
    layer_params = (
        rows(norm1_g), (w_in * _in_proj_column_scale()).astype(BF16), rows(ret_gn_g),
        jnp.broadcast_to(conv_w.reshape(depth, CONV_WIDTH, 1, MIX).astype(F32),
                         (depth, CONV_WIDTH, SUBLANES, MIX)),
        rows(conv_b), rows(conv_ln_g), rows(conv_ln_b), rows(gmlp_ln_g), rows(gmlp_ln_b),
        gmlp_ws.astype(F32),
        jnp.broadcast_to(gmlp_bs.astype(F32)[..., None], (depth, GMLP_GROUPS, CHUNK, 128)),
        rows(jnp.tile(swa_q_g, (1, SWA_HEADS))) * (SWA_HEAD_DIM ** -0.5),
        rows(jnp.tile(swa_k_g, (1, SWA_KV_HEADS))),
        w_branch.astype(BF16), (w_out * 0.5).astype(BF16))
    ffn_scale = jnp.where(jnp.arange(2 * D_FF) < D_FF, 0.5, 1.0).astype(F32)
    n2g, wfi, wfo = rows(norm2_g), (w_ffn_in * ffn_scale).astype(BF16), w_ffn_out.astype(BF16)
    for l in range(depth):
        x = _mixer_call(l, x, tables, bucket, hsum, rel_bias.astype(F32), swa_sinks.astype(F32), layer_params)
        x = _ffn_call(l, x.reshape(B * S, D), n2g, wfi, wfo).reshape(B, S, D)
    return x
```

```python
import functools
import math

import jax
import jax.numpy as jnp
from jax import lax
from jax.experimental import pallas as pl
from jax.experimental.pallas import tpu as pltpu

D_MODEL = 1024
SEQ = 8192
EPS = 1e-6
MIX = 512
N_BRANCH = 4
CHUNK = 128
RET_HEADS = 4
RET_DK = 128
ROPE_BASE = 10000.0
CONV_WIDTH = 31
CONV_TAIL = 32
CONV_ROWS = 8
SUBLANES = 8
GMLP_GROUPS = 4
SWA_HEADS = 8
SWA_KV_HEADS = 2
SWA_GROUP = SWA_HEADS // SWA_KV_HEADS
SWA_HEAD_DIM = 64
SWA_WINDOW = 128
REL_BUCKETS = 32
REL_MAX_DIST = 128
D_FF = 2816
OFF_RET = 0
OFF_CONV = 2048
OFF_GMLP = 3072
OFF_SWA = 4096
OFF_GATE = 4864
IN_COLS = 8960

MIXER_TILE = 256
FFN_TILE = 512
COL_BLOCK = 256
VMEM_LIMIT_BYTES = 58 * 1024 * 1024

F32 = jnp.float32
BF16 = jnp.bfloat16


def _mm(a, b):
    return jnp.dot(a, b, preferred_element_type=F32)


def _sigmoid(x):
    return 0.5 * jnp.tanh(0.5 * x) + 0.5


def _silu(x):
    return x * _sigmoid(x)


def _silu_of_half(h):
    return h * jnp.tanh(h) + h


def _zero_of(v):
    bits = pltpu.bitcast(v, jnp.uint32)
    bits = lax.shift_right_logical(lax.shift_right_logical(bits, jnp.uint32(16)), jnp.uint32(16))
    return pltpu.bitcast(bits, F32)


def _gelu_tanh(x):
    c = math.sqrt(2.0 / math.pi)
    return x * (0.5 * (1.0 + jnp.tanh(c * (x + 0.044715 * (x * x * x)))))


def _rms_norm(x, g):
    return x * lax.rsqrt(jnp.mean(x * x, axis=-1, keepdims=True) + EPS) * g


def _layer_norm(x, g, b):
    mu = jnp.mean(x, axis=-1, keepdims=True)
    d = x - mu
    var = jnp.mean(d * d, axis=-1, keepdims=True)
    return d * lax.rsqrt(var + EPS) * g + b


def _interleave(*streams):
    keyed = []
    for s, stream in enumerate(streams):
        keyed += [((i + 0.5) / len(stream), s, i, fn) for i, fn in enumerate(stream)]
    for _, _, _, fn in sorted(keyed, key=lambda e: e[:3]):
        fn()


def _mixer_kernel(layer, x_ref, cq_ref, sq_ref, ck_ref, sk_ref, decay_ref, xi_ref, zetat_ref, cdec_ref,
                  n1g_ref, win_ref, gng_ref, cw_ref, cb_ref, clg_ref, clb_ref, glg_ref, glb_ref,
                  ws_ref, bs_ref, qg_ref, kg_ref, sinks_ref, rel_ref, bucket_ref, hsum_ref, wbr_ref, wout_ref,
                  o_ref, state_ref, ubuf_ref, ushift_ref, yb_ref, gates_ref, kprev_ref, vprev_ref, bias_ref):
    ts = x_ref.shape[1]
    n_chunks = ts // CHUNK
    t = pl.program_id(1)

    @pl.when((pl.program_id(0) == 0) & (t == 0))
    def _():
        bucket = bucket_ref[...]
        for h in range(SWA_HEADS):
            bias_ref[h] = lax.fori_loop(
                0, REL_BUCKETS, lambda k, acc: jnp.where(bucket == k, rel_ref[k, h], acc),
                jnp.full(bucket.shape, -jnp.inf, F32))

    @pl.when(t == 0)
    def _():
        state_ref[...] = jnp.zeros_like(state_ref)
        ubuf_ref[0:CONV_TAIL, :] = jnp.zeros((CONV_TAIL, MIX), F32)
        kprev_ref[...] = jnp.zeros_like(kprev_ref)
        vprev_ref[...] = jnp.zeros_like(vprev_ref)

    x = x_ref[0]
    hb = _rms_norm(x, n1g_ref[...]).astype(BF16)
    z = {}

    def in_proj(off, width):
        return _mm(hb, win_ref[:, off:off + width])

    def proj_task(name, off, width):
        def run():
            z[name] = in_proj(off, width)
        return run

    def gate_task(i):
        def run():
            c0 = i * COL_BLOCK
            gates_ref[:, c0:c0 + COL_BLOCK] = jnp.tanh(in_proj(OFF_GATE + c0, COL_BLOCK))
        return run

    gate_tasks = [gate_task(i) for i in range(N_BRANCH * D_MODEL // COL_BLOCK)]

    mix = [None] * (D_MODEL // COL_BLOCK)

    def branch_tasks(b, get_y):
        def task(j):
            def run():
                if j == 0:
                    z["ybf", b] = get_y().astype(BF16)
                c0 = j * COL_BLOCK
                br = _mm(z["ybf", b], wbr_ref[b, :, c0:c0 + COL_BLOCK])
                term = gates_ref[:, b * D_MODEL + c0:b * D_MODEL + c0 + COL_BLOCK] * br + br
                mix[j] = term if mix[j] is None else mix[j] + term
            return run
        return [task(j) for j in range(D_MODEL // COL_BLOCK)]

    zb = in_proj(OFF_CONV, 2 * MIX)
    ubuf_ref[CONV_TAIL:CONV_TAIL + ts, :] = zb[:, :MIX] * jnp.tanh(zb[:, MIX:]) + zb[:, :MIX]
    base = CONV_TAIL - (CONV_WIDTH - 1)
    taps = []
    shift_tasks = []
    for b in range(SUBLANES):
        n_a = (CONV_WIDTH - b + SUBLANES - 1) // SUBLANES
        off = base + b
        if off % SUBLANES == 0:
            src, src_off = ubuf_ref, off
        else:
            def shift(b=b, off=off, rows=ts + SUBLANES * (n_a - 1)):
                ushift_ref[b, 0:rows, :] = ubuf_ref[off:off + rows, :]
            shift_tasks.append(shift)
            src, src_off = ushift_ref.at[b], 0
        taps += [(SUBLANES * a + b, src, src_off + SUBLANES * a) for a in range(n_a)]

    _interleave(shift_tasks,
                [proj_task("qk", OFF_RET, 1024), proj_task("vg", OFF_RET + 1024, 1024),
                 proj_task("c", OFF_GMLP, 1024), proj_task("d", OFF_SWA, 768)])

    def gmlp_front():
        zc = _gelu_tanh(z["c"])
        z["ug"] = zc[:, :MIX]
        z["vg16"] = _layer_norm(zc[:, MIX:], glg_ref[...], glb_ref[...]).astype(BF16)

    def swa_front():
        zd = z["d"]
        qd, kd = zd[:, :512], zd[:, 512:640]
        hsum = hsum_ref[...]
        z["qn"] = qd * lax.rsqrt(_mm((qd * qd).astype(BF16), hsum) + EPS) * qg_ref[...]
        z["kn"] = kd * lax.rsqrt(_mm((kd * kd).astype(BF16), hsum[0:128, 0:128]) + EPS) * kg_ref[...]
        z["vd"] = zd[:, 640:768]

    n_gate = len(gate_tasks)
    _interleave([gmlp_front, swa_front], gate_tasks[0:n_gate // 4])

    cb, clg, clb = cb_ref[...], clg_ref[...], clb_ref[...]

    conv_prev = []

    def conv_task(r0):
        def run():
            acc = cb + _zero_of(conv_prev[-1]) if conv_prev else cb
            for j, src, off in taps:
                w = jnp.concatenate([cw_ref[j]] * (CONV_ROWS // SUBLANES), axis=0)
                acc = acc + w * src[off + r0:off + r0 + CONV_ROWS, :]
            conv_prev.append(acc)
            yb_ref[r0:r0 + CONV_ROWS, :] = _silu(_layer_norm(acc, clg, clb)).astype(BF16)
        return run

    _interleave([conv_task(r0) for r0 in range(0, ts, CONV_ROWS)], gate_tasks[n_gate // 4:3 * n_gate // 4])
    ubuf_ref[0:CONV_TAIL, :] = ubuf_ref[ts:ts + CONV_TAIL, :]

    gng = gng_ref[...]
    cdec = cdec_ref[...]
    ya_chunks = []
    ret = {}

    def ret_task(c, hd):
        def run():
            r = slice(c * CHUNK, (c + 1) * CHUNK)
            l = slice(hd * RET_DK, (hd + 1) * RET_DK)
            if hd == 0:
                ret["st"] = state_ref[...]
                ret["heads"], ret["new_state"] = [], []
            st = ret["st"]
            q = z["qk"][r, hd * 128:(hd + 1) * 128]
            k = z["qk"][r, 512 + hd * 128:512 + (hd + 1) * 128]
            v = z["vg"][r, hd * 128:(hd + 1) * 128]
            q = q * cq_ref[r, :] + pltpu.roll(q, 64, 1) * sq_ref[r, :]
            k = k * ck_ref[r, :] + pltpu.roll(k, 64, 1) * sk_ref[r, :]
            vb = v.astype(BF16)
            kt = k.T
            s = _mm(q.astype(BF16), kt.astype(BF16)) * decay_ref[hd]
            lhs = jnp.concatenate([s.astype(BF16), (q * xi_ref[:, l]).astype(BF16)], axis=1)
            rhs = jnp.concatenate([vb, st[:, l].astype(BF16)], axis=0)
            y = _mm(lhs, rhs)
            kzt = (kt * zetat_ref[hd]).astype(BF16)
            ret["new_state"].append(cdec[:, l] * st[:, l] + _mm(kzt, vb))
            mu = jnp.mean(y, axis=-1, keepdims=True)
            d = y - mu
            var = jnp.mean(d * d, axis=-1, keepdims=True)
            ret["heads"].append(d * lax.rsqrt(var + EPS))
            if hd == RET_HEADS - 1:
                state_ref[...] = jnp.concatenate(ret["new_state"], axis=1)
                yn = jnp.concatenate(ret["heads"], axis=1) * gng
                ya_chunks.append(_silu_of_half(z["vg"][r, 512:1024]) * yn)
        return run

    tril = (lax.broadcasted_iota(jnp.int32, (CHUNK, CHUNK), 0)
            >= lax.broadcasted_iota(jnp.int32, (CHUNK, CHUNK), 1))
    s_groups = []

    def gmlp_task(g):
        def run():
            wsm = jnp.where(tril, ws_ref[g], 0.0).astype(BF16)
            vg = z["vg16"]
            rhs = jnp.concatenate([vg[c * CHUNK:(c + 1) * CHUNK, g * 128:(g + 1) * 128]
                                   for c in range(n_chunks)], axis=1)
            sg = _mm(wsm, rhs)
            bsg = bs_ref[g]
            s_groups.append(jnp.concatenate([sg[:, c * 128:(c + 1) * 128] + bsg
                                             for c in range(n_chunks)], axis=0))
        return run

    lane_lo = lax.broadcasted_iota(jnp.int32, (1, 128), 1) < SWA_HEAD_DIM
    first = jnp.where(t == 0, -jnp.inf, 0.0)
    yd_chunks = []
    att = {}

    def attn_task(c, j):
        def run():
            r = slice(c * CHUNK, (c + 1) * CHUNK)
            if j == 0:
                kc, vc = z["kn"][r, :], z["vd"][r, :]
                kcat = jnp.concatenate([kprev_ref[...], kc], axis=0)
                vcat = jnp.concatenate([vprev_ref[...], vc], axis=0)
                kprev_ref[...] = kc
                vprev_ref[...] = vc
                vroll = pltpu.roll(vcat, 64, 1)
                kt = kcat.T
                kt0, kt1 = kt[0:SWA_HEAD_DIM, :], kt[SWA_HEAD_DIM:2 * SWA_HEAD_DIM, :]
                zero = jnp.zeros_like(kt0)
                att["k"] = [[jnp.concatenate([kt0, zero], axis=0).astype(BF16),
                             jnp.concatenate([zero, kt0], axis=0).astype(BF16)],
                            [jnp.concatenate([kt1, zero], axis=0).astype(BF16),
                             jnp.concatenate([zero, kt1], axis=0).astype(BF16)]]
                att["v"] = [[jnp.where(lane_lo, vcat, 0.0).astype(BF16), jnp.where(lane_lo, 0.0, vroll).astype(BF16)],
                            [jnp.where(lane_lo, vroll, 0.0).astype(BF16), jnp.where(lane_lo, 0.0, vcat).astype(BF16)]]
                att["outs"] = []
            kv = (2 * j) // SWA_GROUP
            qblk = z["qn"][r, j * 128:(j + 1) * 128].astype(BF16)
            ps, rinv = [], []
            for pos in range(2):
                hidx = 2 * j + pos
                s = _mm(qblk, att["k"][kv][pos]) + bias_ref[hidx]
                if c == 0:
                    s = jnp.concatenate([s[:, :CHUNK] + first, s[:, CHUNK:]], axis=1)
                sink = sinks_ref[layer, hidx]
                m = jnp.maximum(jnp.max(s, axis=-1, keepdims=True), sink)
                p = jnp.exp(s - m)
                den = jnp.sum(p, axis=-1, keepdims=True) + jnp.exp(sink - m)
                ps.append(p.astype(BF16))
                rinv.append(1.0 / den)
            o = _mm(jnp.concatenate(ps, axis=1),
                    jnp.concatenate([att["v"][kv][0], att["v"][kv][1]], axis=0))
            att["outs"].append(o * jnp.where(lane_lo, rinv[0], rinv[1]))
            if j == SWA_HEADS // 2 - 1:
                yd_chunks.append(jnp.concatenate(att["outs"], axis=1))
        return run

    _interleave([ret_task(c, hd) for c in range(n_chunks) for hd in range(RET_HEADS)],
                gate_tasks[3 * n_gate // 4:] + [gmlp_task(g) for g in range(GMLP_GROUPS)])
    _interleave([attn_task(c, j) for c in range(n_chunks) for j in range(SWA_HEADS // 2)],
                branch_tasks(0, lambda: jnp.concatenate(ya_chunks, axis=0))
                + branch_tasks(1, lambda: yb_ref[...])
                + branch_tasks(2, lambda: z["ug"] * jnp.concatenate(s_groups, axis=1)))
    for task in branch_tasks(3, lambda: jnp.concatenate(yd_chunks, axis=0)):
        task()

    o_ref[0] = x + _mm(jnp.concatenate(mix, axis=1).astype(BF16), wout_ref[...])


def _ffn_kernel(x_ref, g_ref, win_ref, wout_ref, o_ref):
    x = x_ref[...]
    hb = _rms_norm(x, g_ref[...]).astype(BF16)
    gh = _mm(hb, win_ref[:, :D_FF])
    up = _mm(hb, win_ref[:, D_FF:])
    o_ref[...] = x + _mm((_silu_of_half(gh) * up).astype(BF16), wout_ref[...])


def _const_spec(shape):
    nd = len(shape)
    return pl.BlockSpec(shape, lambda *_: (0,) * nd, pipeline_mode=pl.Buffered(1))


def _layer_spec(arr, layer):
    nd = arr.ndim - 1
    return pl.BlockSpec((None,) + arr.shape[1:], lambda *_: (layer,) + (0,) * nd, pipeline_mode=pl.Buffered(1))


def _t5_bucket(dist):
    max_exact = REL_BUCKETS // 2
    d = jnp.maximum(dist, 1).astype(F32)
    large = max_exact + (jnp.log(d / max_exact) / math.log(REL_MAX_DIST / max_exact)
                         * (REL_BUCKETS - max_exact)).astype(jnp.int32)
    large = jnp.minimum(large, REL_BUCKETS - 1)
    return jnp.where(dist < max_exact, dist, large)


def _position_tables():
    half = RET_DK // 2
    inv = ROPE_BASE ** (-jnp.arange(half, dtype=F32) / half)
    ang = jnp.arange(SEQ).astype(F32)[:, None] * inv[None, :]
    cos, sin = jnp.cos(ang), jnp.sin(ang)
    cos2 = jnp.concatenate([cos, cos], axis=1)
    sin2 = jnp.concatenate([-sin, sin], axis=1)
    kscale = RET_DK ** -0.5
    gamma = 1.0 - 2.0 ** (-5.0 - jnp.arange(RET_HEADS, dtype=F32))
    log_g = jnp.log(gamma)
    idx = jnp.arange(CHUNK, dtype=F32)
    diff = idx[:, None] - idx[None, :]
    decay = jnp.where(diff >= 0, jnp.exp(log_g[:, None, None] * jnp.maximum(diff, 0.0)), 0.0)
    xi = jnp.repeat(jnp.exp(log_g[None, :] * (idx[:, None] + 1.0)), RET_DK, axis=1)
    zeta = jnp.exp(log_g[None, :] * (CHUNK - 1.0 - idx[:, None]))
    zetat = jnp.broadcast_to(zeta.T[:, None, :], (RET_HEADS, RET_DK, CHUNK))
    cdec = jnp.repeat(jnp.exp(log_g * CHUNK), RET_DK)[None, :]
    return cos2, sin2, cos2 * kscale, sin2 * kscale, decay, xi, zetat, cdec


def _bucket_table():
    qi = jnp.arange(CHUNK)[:, None] + CHUNK
    kj = jnp.arange(2 * CHUNK)[None, :]
    dist = qi - kj
    in_win = (dist >= 0) & (dist < SWA_WINDOW)
    return jnp.where(in_win, _t5_bucket(jnp.maximum(dist, 0)), -1).astype(jnp.int32)


def _in_proj_column_scale():
    col = jnp.arange(IN_COLS)
    halved = (((col >= OFF_CONV) & (col < OFF_GMLP))
              | ((col >= OFF_RET + 3 * MIX) & (col < OFF_CONV))
              | (col >= OFF_GATE))
    return jnp.where(halved, 0.5, 1.0).astype(F32)


def _mixer_call(layer, x, tables, bucket, hsum, rel_bias, sinks, layer_params):
    B, S, D = x.shape
    ts = MIXER_TILE
    pos_spec = pl.BlockSpec((ts, 128), lambda b, t: (t, 0))
    smem = pl.BlockSpec(memory_space=pltpu.SMEM)
    n1g, win, gng, cw, cb, clg, clb, glg, glb, ws, bs, qg, kg, wbr, wout = layer_params
    lspec = lambda a: _layer_spec(a, layer)
    in_specs = (
        [pl.BlockSpec((1, ts, D), lambda b, t: (b, t, 0))]
        + [pos_spec] * 4 + [_const_spec(a.shape) for a in tables[4:]]
        + [lspec(a) for a in (n1g, win, gng, cw, cb, clg, clb, glg, glb, ws, bs, qg, kg)]
        + [smem, smem, _const_spec(bucket.shape), _const_spec(hsum.shape), lspec(wbr), lspec(wout)])
    return pl.pallas_call(
        functools.partial(_mixer_kernel, layer),
        grid=(B, S // ts),
        in_specs=in_specs,
        out_specs=pl.BlockSpec((1, ts, D), lambda b, t: (b, t, 0)),
        out_shape=jax.ShapeDtypeStruct(x.shape, x.dtype),
        scratch_shapes=[
            pltpu.VMEM((RET_DK, MIX), F32),
            pltpu.VMEM((CONV_TAIL + ts, MIX), F32),
            pltpu.VMEM((SUBLANES, ts + CONV_TAIL - SUBLANES, MIX), F32),
            pltpu.VMEM((ts, MIX), BF16),
            pltpu.VMEM((ts, N_BRANCH * D_MODEL), F32),
            pltpu.VMEM((CHUNK, 128), F32),
            pltpu.VMEM((CHUNK, 128), F32),
            pltpu.VMEM((SWA_HEADS, CHUNK, 2 * CHUNK), F32),
        ],
        compiler_params=pltpu.CompilerParams(
            dimension_semantics=("arbitrary", "arbitrary"),
            vmem_limit_bytes=VMEM_LIMIT_BYTES),
        name="mixer",
    )(x, *tables, n1g, win, gng, cw, cb, clg, clb, glg, glb, ws, bs, qg, kg,
      sinks, rel_bias, bucket, hsum, wbr, wout)


def _ffn_call(layer, x2, g, win, wout):
    T, D = x2.shape
    ts = FFN_TILE
    return pl.pallas_call(
        _ffn_kernel,
        grid=(T // ts,),
        in_specs=[pl.BlockSpec((ts, D), lambda i: (i, 0)),
                  _layer_spec(g, layer), _layer_spec(win, layer), _layer_spec(wout, layer)],
        out_specs=pl.BlockSpec((ts, D), lambda i: (i, 0)),
        out_shape=jax.ShapeDtypeStruct(x2.shape, x2.dtype),
        compiler_params=pltpu.CompilerParams(
            dimension_semantics=("arbitrary",),
            vmem_limit_bytes=VMEM_LIMIT_BYTES),
        name="ffn",
    )(x2, g, win, wout)


def kernel(x, rel_bias, norm1_g, w_in, ret_gn_g, conv_w, conv_b, conv_ln_g, conv_ln_b, gmlp_ln_g, gmlp_ln_b, gmlp_ws, gmlp_bs, swa_q_g, swa_k_g, swa_sinks, w_branch, w_out, norm2_g, w_ffn_in, w_ffn_out):
    B, S, D = x.shape
    depth = w_in.shape[0]
    tables = _position_tables()
    bucket = _bucket_table()
    head_id = jnp.arange(MIX) // SWA_HEAD_DIM
    hsum = jnp.where(head_id[:, None] == head_id[None, :], 1.0 / SWA_HEAD_DIM, 0.0).astype(BF16)
    rows = lambda v: v.reshape(depth, 1, -1).astype(F32)
    layer_params = (
        rows(norm1_g), (w_in * _in_proj_column_scale()).astype(BF16), rows(ret_gn_g),
        jnp.broadcast_to(conv_w.reshape(depth, CONV_WIDTH, 1, MIX).astype(F32),
                         (depth, CONV_WIDTH, SUBLANES, MIX)),
        rows(conv_b), rows(conv_ln_g), rows(conv_ln_b), rows(gmlp_ln_g), rows(gmlp_ln_b),
        gmlp_ws.astype(F32),
        jnp.broadcast_to(gmlp_bs.astype(F32)[..., None], (depth, GMLP_GROUPS, CHUNK, 128)),
        rows(jnp.tile(swa_q_g, (1, SWA_HEADS))) * (SWA_HEAD_DIM ** -0.5),
        rows(jnp.tile(swa_k_g, (1, SWA_KV_HEADS))),
        w_branch.astype(BF16), (w_out * 0.5).astype(BF16))
    ffn_scale = jnp.where(jnp.arange(2 * D_FF) < D_FF, 0.5, 1.0).astype(F32)
    n2g, wfi, wfo = rows(norm2_g), (w_ffn_in * ffn_scale).astype(BF16), w_ffn_out.astype(BF16)
    for l in range(depth):
        x = _mixer_call(l, x, tables, bucket, hsum, rel_bias.astype(F32), swa_sinks.astype(F32), layer_params)
        x = _ffn_call(l, x.reshape(B * S, D), n2g, wfi, wfo).reshape(B, S, D)
    return x
```

```python
import functools
import math

import jax
import jax.numpy as jnp
from jax import lax
from jax.experimental import pallas as pl
from jax.experimental.pallas import tpu as pltpu

D_MODEL = 1024
SEQ = 8192
EPS = 1e-6
MIX = 512
N_BRANCH = 4
CHUNK = 128
RET_HEADS = 4
RET_DK = 128
ROPE_BASE = 10000.0
CONV_WIDTH = 31
CONV_TAIL = 32
CONV_ROWS = 32
SUBLANES = 8
GMLP_GROUPS = 4
SWA_HEADS = 8
SWA_KV_HEADS = 2
SWA_GROUP = SWA_HEADS // SWA_KV_HEADS
SWA_HEAD_DIM = 64
SWA_WINDOW = 128
REL_BUCKETS = 32
REL_MAX_DIST = 128
D_FF = 2816
OFF_RET = 0
OFF_CONV = 2048
OFF_GMLP = 3072
OFF_SWA = 4096
OFF_GATE = 4864
IN_COLS = 8960

MIXER_TILE = 256
FFN_TILE = 512
COL_BLOCK = 256
VMEM_LIMIT_BYTES = 58 * 1024 * 1024

F32 = jnp.float32
BF16 = jnp.bfloat16


def _mm(a, b):
    return jnp.dot(a, b, preferred_element_type=F32)


def _sigmoid(x):
    return 0.5 * jnp.tanh(0.5 * x) + 0.5


def _silu(x):
    return x * _sigmoid(x)


def _silu_of_half(h):
    return h * jnp.tanh(h) + h


def _gelu_tanh(x):
    c = math.sqrt(2.0 / math.pi)
    return x * (0.5 * (1.0 + jnp.tanh(c * (x + 0.044715 * (x * x * x)))))


def _rms_norm(x, g):
    return x * lax.rsqrt(jnp.mean(x * x, axis=-1, keepdims=True) + EPS) * g


def _layer_norm(x, g, b):
    mu = jnp.mean(x, axis=-1, keepdims=True)
    d = x - mu
    var = jnp.mean(d * d, axis=-1, keepdims=True)
    return d * lax.rsqrt(var + EPS) * g + b


def _interleave(*streams):
    keyed = []
    for s, stream in enumerate(streams):
        keyed += [((i + 0.5) / len(stream), s, i, fn) for i, fn in enumerate(stream)]
    for _, _, _, fn in sorted(keyed, key=lambda e: e[:3]):
        fn()


def _mixer_kernel(layer, x_ref, cq_ref, sq_ref, ck_ref, sk_ref, decay_ref, xi_ref, zetat_ref, cdec_ref,
                  n1g_ref, win_ref, gng_ref, cw_ref, cb_ref, clg_ref, clb_ref, glg_ref, glb_ref,
                  ws_ref, bs_ref, qg_ref, kg_ref, sinks_ref, rel_ref, bucket_ref, hsum_ref, wbr_ref, wout_ref,
                  o_ref, state_ref, ubuf_ref, ushift_ref, yb_ref, gates_ref, kprev_ref, vprev_ref, bias_ref):
    ts = x_ref.shape[1]
    n_chunks = ts // CHUNK
    t = pl.program_id(1)

    @pl.when((pl.program_id(0) == 0) & (t == 0))
    def _():
        bucket = bucket_ref[...]
        for h in range(SWA_HEADS):
            bias_ref[h] = lax.fori_loop(
                0, REL_BUCKETS, lambda k, acc: jnp.where(bucket == k, rel_ref[k, h], acc),
                jnp.full(bucket.shape, -jnp.inf, F32))

    @pl.when(t == 0)
    def _():
        state_ref[...] = jnp.zeros_like(state_ref)
        ubuf_ref[0:CONV_TAIL, :] = jnp.zeros((CONV_TAIL, MIX), F32)
        kprev_ref[...] = jnp.zeros_like(kprev_ref)
        vprev_ref[...] = jnp.zeros_like(vprev_ref)

    x = x_ref[0]
    hb = _rms_norm(x, n1g_ref[...]).astype(BF16)
    z = {}

    def in_proj(off, width):
        return _mm(hb, win_ref[:, off:off + width])

    def proj_task(name, off, width):
        def run():
            z[name] = in_proj(off, width)
        return run

    def gate_task(i):
        def run():
            c0 = i * COL_BLOCK
            gates_ref[:, c0:c0 + COL_BLOCK] = jnp.tanh(in_proj(OFF_GATE + c0, COL_BLOCK))
        return run

    gate_tasks = [gate_task(i) for i in range(N_BRANCH * D_MODEL // COL_BLOCK)]

    mix = [None] * (D_MODEL // COL_BLOCK)

    def branch_tasks(b, get_y):
        def task(j):
            def run():
                if j == 0:
                    z["ybf", b] = get_y().astype(BF16)
                c0 = j * COL_BLOCK
                br = _mm(z["ybf", b], wbr_ref[b, :, c0:c0 + COL_BLOCK])
                term = gates_ref[:, b * D_MODEL + c0:b * D_MODEL + c0 + COL_BLOCK] * br + br
                mix[j] = term if mix[j] is None else mix[j] + term
            return run
        return [task(j) for j in range(D_MODEL // COL_BLOCK)]

    zb = in_proj(OFF_CONV, 2 * MIX)
    ubuf_ref[CONV_TAIL:CONV_TAIL + ts, :] = zb[:, :MIX] * jnp.tanh(zb[:, MIX:]) + zb[:, :MIX]
    base = CONV_TAIL - (CONV_WIDTH - 1)
    taps = []
    shift_tasks = []
    for b in range(SUBLANES):
        n_a = (CONV_WIDTH - b + SUBLANES - 1) // SUBLANES
        off = base + b
        if off % SUBLANES == 0:
            src, src_off = ubuf_ref, off
        else:
            def shift(b=b, off=off, rows=ts + SUBLANES * (n_a - 1)):
                ushift_ref[b, 0:rows, :] = ubuf_ref[off:off + rows, :]
            shift_tasks.append(shift)
            src, src_off = ushift_ref.at[b], 0
        taps += [(SUBLANES * a + b, src, src_off + SUBLANES * a) for a in range(n_a)]

    _interleave(shift_tasks,
                [proj_task("qk", OFF_RET, 1024), proj_task("vg", OFF_RET + 1024, 1024),
                 proj_task("c", OFF_GMLP, 1024), proj_task("d", OFF_SWA, 768)])

    def gmlp_front():
        zc = _gelu_tanh(z["c"])
        z["ug"] = zc[:, :MIX]
        z["vg16"] = _layer_norm(zc[:, MIX:], glg_ref[...], glb_ref[...]).astype(BF16)

    def swa_front():
        zd = z["d"]
        qd, kd = zd[:, :512], zd[:, 512:640]
        hsum = hsum_ref[...]
        z["qn"] = qd * lax.rsqrt(_mm((qd * qd).astype(BF16), hsum) + EPS) * qg_ref[...]
        z["kn"] = kd * lax.rsqrt(_mm((kd * kd).astype(BF16), hsum[0:128, 0:128]) + EPS) * kg_ref[...]
        z["vd"] = zd[:, 640:768]

    n_gate = len(gate_tasks)
    _interleave([gmlp_front, swa_front], gate_tasks[0:n_gate // 4])

    cb, clg, clb = cb_ref[...], clg_ref[...], clb_ref[...]

    def conv_task(r0):
        def run():
            acc = cb
            for j, src, off in taps:
                w = jnp.concatenate([cw_ref[j]] * (CONV_ROWS // SUBLANES), axis=0)
                acc = acc + w * src[off + r0:off + r0 + CONV_ROWS, :]
            yb_ref[r0:r0 + CONV_ROWS, :] = _silu(_layer_norm(acc, clg, clb)).astype(BF16)
        return run

    _interleave([conv_task(r0) for r0 in range(0, ts, CONV_ROWS)], gate_tasks[n_gate // 4:3 * n_gate // 4])
    ubuf_ref[0:CONV_TAIL, :] = ubuf_ref[ts:ts + CONV_TAIL, :]

    gng = gng_ref[...]
    cdec = cdec_ref[...]
    ya_chunks = []
    ret = {}

    def ret_scores_task(c):
        def run():
            r = slice(c * CHUNK, (c + 1) * CHUNK)
            st = state_ref[...]
            new_state = []
            for hd in range(RET_HEADS):
                l = slice(hd * RET_DK, (hd + 1) * RET_DK)
                q = z["qk"][r, hd * 128:(hd + 1) * 128]
                k = z["qk"][r, 512 + hd * 128:512 + (hd + 1) * 128]
                v = z["vg"][r, hd * 128:(hd + 1) * 128]
                q = q * cq_ref[r, :] + pltpu.roll(q, 64, 1) * sq_ref[r, :]
                k = k * ck_ref[r, :] + pltpu.roll(k, 64, 1) * sk_ref[r, :]
                vb = v.astype(BF16)
                kt = k.T
                s = _mm(q.astype(BF16), kt.astype(BF16)) * decay_ref[hd]
                ret[c, hd] = (jnp.concatenate([s.astype(BF16), (q * xi_ref[:, l]).astype(BF16)], axis=1),
                              jnp.concatenate([vb, st[:, l].astype(BF16)], axis=0))
                kzt = (kt * zetat_ref[hd]).astype(BF16)
                new_state.append(cdec[:, l] * st[:, l] + _mm(kzt, vb))
            state_ref[...] = jnp.concatenate(new_state, axis=1)
        return run

    def ret_output_task(c):
        def run():
            r = slice(c * CHUNK, (c + 1) * CHUNK)
            heads = []
            for hd in range(RET_HEADS):
                y = _mm(*ret[c, hd])
                mu = jnp.mean(y, axis=-1, keepdims=True)
                d = y - mu
                var = jnp.mean(d * d, axis=-1, keepdims=True)
                heads.append(d * lax.rsqrt(var + EPS))
            yn = jnp.concatenate(heads, axis=1) * gng
            ya_chunks.append(_silu_of_half(z["vg"][r, 512:1024]) * yn)
        return run

    tril = (lax.broadcasted_iota(jnp.int32, (CHUNK, CHUNK), 0)
            >= lax.broadcasted_iota(jnp.int32, (CHUNK, CHUNK), 1))
    s_groups = []

    def gmlp_task(g):
        def run():
            wsm = jnp.where(tril, ws_ref[g], 0.0).astype(BF16)
            vg = z["vg16"]
            rhs = jnp.concatenate([vg[c * CHUNK:(c + 1) * CHUNK, g * 128:(g + 1) * 128]
                                   for c in range(n_chunks)], axis=1)
            sg = _mm(wsm, rhs)
            bsg = bs_ref[g]
            s_groups.append(jnp.concatenate([sg[:, c * 128:(c + 1) * 128] + bsg
                                             for c in range(n_chunks)], axis=0))
        return run

    lane_lo = lax.broadcasted_iota(jnp.int32, (1, 128), 1) < SWA_HEAD_DIM
    first = jnp.where(t == 0, -jnp.inf, 0.0)
    yd_chunks = []
    att = {}

    def attn_scores_task(c):
        def run():
            r = slice(c * CHUNK, (c + 1) * CHUNK)
            kc, vc = z["kn"][r, :], z["vd"][r, :]
            kcat = jnp.concatenate([kprev_ref[...], kc], axis=0)
            vcat = jnp.concatenate([vprev_ref[...], vc], axis=0)
            kprev_ref[...] = kc
            vprev_ref[...] = vc
            vroll = pltpu.roll(vcat, 64, 1)
            kt = kcat.T
            kt0, kt1 = kt[0:SWA_HEAD_DIM, :], kt[SWA_HEAD_DIM:2 * SWA_HEAD_DIM, :]
            zero = jnp.zeros_like(kt0)
            kplace = [[jnp.concatenate([kt0, zero], axis=0).astype(BF16),
                       jnp.concatenate([zero, kt0], axis=0).astype(BF16)],
                      [jnp.concatenate([kt1, zero], axis=0).astype(BF16),
                       jnp.concatenate([zero, kt1], axis=0).astype(BF16)]]
            att["v", c] = [[jnp.where(lane_lo, vcat, 0.0).astype(BF16), jnp.where(lane_lo, 0.0, vroll).astype(BF16)],
                           [jnp.where(lane_lo, vroll, 0.0).astype(BF16), jnp.where(lane_lo, 0.0, vcat).astype(BF16)]]
            for hidx in range(SWA_HEADS):
                j, pos = hidx // 2, hidx % 2
                kv = hidx // SWA_GROUP
                qblk = z["qn"][r, j * 128:(j + 1) * 128].astype(BF16)
                s = _mm(qblk, kplace[kv][pos]) + bias_ref[hidx]
                if c == 0:
                    s = jnp.concatenate([s[:, :CHUNK] + first, s[:, CHUNK:]], axis=1)
                att["s", c, hidx] = s
        return run

    def attn_softmax_task(c):
        def run():
            for hidx in range(SWA_HEADS):
                s = att["s", c, hidx]
                sink = sinks_ref[layer, hidx]
                m = jnp.maximum(jnp.max(s, axis=-1, keepdims=True), sink)
                p = jnp.exp(s - m)
                den = jnp.sum(p, axis=-1, keepdims=True) + jnp.exp(sink - m)
                att["p", c, hidx] = (p.astype(BF16), 1.0 / den)
        return run

    def attn_output_task(c):
        def run():
            outs = []
            for j in range(SWA_HEADS // 2):
                kv = (2 * j) // SWA_GROUP
                (p0, r0), (p1, r1) = att["p", c, 2 * j], att["p", c, 2 * j + 1]
                vplace = att["v", c][kv]
                o = _mm(jnp.concatenate([p0, p1], axis=1),
                        jnp.concatenate([vplace[0], vplace[1]], axis=0))
                outs.append(o * jnp.where(lane_lo, r0, r1))
            yd_chunks.append(jnp.concatenate(outs, axis=1))
        return run

    chunks = range(n_chunks)
    _interleave([ret_scores_task(c) for c in chunks] + [ret_output_task(c) for c in chunks],
                gate_tasks[3 * n_gate // 4:] + [gmlp_task(g) for g in range(GMLP_GROUPS)])
    _interleave([attn_scores_task(c) for c in chunks] + [attn_softmax_task(c) for c in chunks]
                + [attn_output_task(c) for c in chunks],
                branch_tasks(0, lambda: jnp.concatenate(ya_chunks, axis=0))
                + branch_tasks(1, lambda: yb_ref[...])
                + branch_tasks(2, lambda: z["ug"] * jnp.concatenate(s_groups, axis=1)))
    for task in branch_tasks(3, lambda: jnp.concatenate(yd_chunks, axis=0)):
        task()

    o_ref[0] = x + _mm(jnp.concatenate(mix, axis=1).astype(BF16), wout_ref[...])


def _ffn_kernel(x_ref, g_ref, win_ref, wout_ref, o_ref):
    x = x_ref[...]
    hb = _rms_norm(x, g_ref[...]).astype(BF16)
    gh = _mm(hb, win_ref[:, :D_FF])
    up = _mm(hb, win_ref[:, D_FF:])
    o_ref[...] = x + _mm((_silu_of_half(gh) * up).astype(BF16), wout_ref[...])


def _const_spec(shape):
    nd = len(shape)
    return pl.BlockSpec(shape, lambda *_: (0,) * nd, pipeline_mode=pl.Buffered(1))


def _layer_spec(arr, layer):
    nd = arr.ndim - 1
    return pl.BlockSpec((None,) + arr.shape[1:], lambda *_: (layer,) + (0,) * nd, pipeline_mode=pl.Buffered(1))


def _t5_bucket(dist):
    max_exact = REL_BUCKETS // 2
    d = jnp.maximum(dist, 1).astype(F32)
    large = max_exact + (jnp.log(d / max_exact) / math.log(REL_MAX_DIST / max_exact)
                         * (REL_BUCKETS - max_exact)).astype(jnp.int32)
    large = jnp.minimum(large, REL_BUCKETS - 1)
    return jnp.where(dist < max_exact, dist, large)


def _position_tables():
    half = RET_DK // 2
    inv = ROPE_BASE ** (-jnp.arange(half, dtype=F32) / half)
    ang = jnp.arange(SEQ).astype(F32)[:, None] * inv[None, :]
    cos, sin = jnp.cos(ang), jnp.sin(ang)
    cos2 = jnp.concatenate([cos, cos], axis=1)
    sin2 = jnp.concatenate([-sin, sin], axis=1)
    kscale = RET_DK ** -0.5
    gamma = 1.0 - 2.0 ** (-5.0 - jnp.arange(RET_HEADS, dtype=F32))
    log_g = jnp.log(gamma)
    idx = jnp.arange(CHUNK, dtype=F32)
    diff = idx[:, None] - idx[None, :]
    decay = jnp.where(diff >= 0, jnp.exp(log_g[:, None, None] * jnp.maximum(diff, 0.0)), 0.0)
    xi = jnp.repeat(jnp.exp(log_g[None, :] * (idx[:, None] + 1.0)), RET_DK, axis=1)
    zeta = jnp.exp(log_g[None, :] * (CHUNK - 1.0 - idx[:, None]))
    zetat = jnp.broadcast_to(zeta.T[:, None, :], (RET_HEADS, RET_DK, CHUNK))
    cdec = jnp.repeat(jnp.exp(log_g * CHUNK), RET_DK)[None, :]
    return cos2, sin2, cos2 * kscale, sin2 * kscale, decay, xi, zetat, cdec


def _bucket_table():
    qi = jnp.arange(CHUNK)[:, None] + CHUNK
    kj = jnp.arange(2 * CHUNK)[None, :]
    dist = qi - kj
    in_win = (dist >= 0) & (dist < SWA_WINDOW)
    return jnp.where(in_win, _t5_bucket(jnp.maximum(dist, 0)), -1).astype(jnp.int32)


def _in_proj_column_scale():
    col = jnp.arange(IN_COLS)
    halved = (((col >= OFF_CONV) & (col < OFF_GMLP))
              | ((col >= OFF_RET + 3 * MIX) & (col < OFF_CONV))
              | (col >= OFF_GATE))
    return jnp.where(halved, 0.5, 1.0).astype(F32)


def _mixer_call(layer, x, tables, bucket, hsum, rel_bias, sinks, layer_params):
    B, S, D = x.shape
    ts = MIXER_TILE
    pos_spec = pl.BlockSpec((ts, 128), lambda b, t: (t, 0))
    smem = pl.BlockSpec(memory_space=pltpu.SMEM)
    n1g, win, gng, cw, cb, clg, clb, glg, glb, ws, bs, qg, kg, wbr, wout = layer_params
    lspec = lambda a: _layer_spec(a, layer)
    in_specs = (
        [pl.BlockSpec((1, ts, D), lambda b, t: (b, t, 0))]
        + [pos_spec] * 4 + [_const_spec(a.shape) for a in tables[4:]]
        + [lspec(a) for a in (n1g, win, gng, cw, cb, clg, clb, glg, glb, ws, bs, qg, kg)]
        + [smem, smem, _const_spec(bucket.shape), _const_spec(hsum.shape), lspec(wbr), lspec(wout)])
    return pl.pallas_call(
        functools.partial(_mixer_kernel, layer),
        grid=(B, S // ts),
        in_specs=in_specs,
        out_specs=pl.BlockSpec((1, ts, D), lambda b, t: (b, t, 0)),
        out_shape=jax.ShapeDtypeStruct(x.shape, x.dtype),
        scratch_shapes=[
            pltpu.VMEM((RET_DK, MIX), F32),
            pltpu.VMEM((CONV_TAIL + ts, MIX), F32),
            pltpu.VMEM((SUBLANES, ts + CONV_TAIL - SUBLANES, MIX), F32),
            pltpu.VMEM((ts, MIX), BF16),
            pltpu.VMEM((ts, N_BRANCH * D_MODEL), F32),
            pltpu.VMEM((CHUNK, 128), F32),
            pltpu.VMEM((CHUNK, 128), F32),
            pltpu.VMEM((SWA_HEADS, CHUNK, 2 * CHUNK), F32),
        ],
        compiler_params=pltpu.CompilerParams(
            dimension_semantics=("arbitrary", "arbitrary"),
            vmem_limit_bytes=VMEM_LIMIT_BYTES),
        name="mixer",
    )(x, *tables, n1g, win, gng, cw, cb, clg, clb, glg, glb, ws, bs, qg, kg,
      sinks, rel_bias, bucket, hsum, wbr, wout)


def _ffn_call(layer, x2, g, win, wout):
    T, D = x2.shape
    ts = FFN_TILE
    return pl.pallas_call(
        _ffn_kernel,
        grid=(T // ts,),
        in_specs=[pl.BlockSpec((ts, D), lambda i: (i, 0)),
                  _layer_spec(g, layer), _layer_spec(win, layer), _layer_spec(wout, layer)],
        out_specs=pl.BlockSpec((ts, D), lambda i: (i, 0)),
        out_shape=jax.ShapeDtypeStruct(x2.shape, x2.dtype),
        compiler_params=pltpu.CompilerParams(
            dimension_semantics=("arbitrary",),
            vmem_limit_bytes=VMEM_LIMIT_BYTES),
        name="ffn",
    )(x2, g, win, wout)


def kernel(x, rel_bias, norm1_g, w_in, ret_gn_g, conv_w, conv_b, conv_ln_g, conv_ln_b, gmlp_ln_g, gmlp_ln_b, gmlp_ws, gmlp_bs, swa_q_g, swa_k_g, swa_sinks, w_branch, w_out, norm2_g, w_ffn_in, w_ffn_out):
    B, S, D = x.shape
    depth = w_in.shape[0]
    tables = _position_tables()
    bucket = _bucket_table()
    head_id = jnp.arange(MIX) // SWA_HEAD_DIM
    hsum = jnp.where(head_id[:, None] == head_id[None, :], 1.0 / SWA_HEAD_DIM, 0.0).astype(BF16)
    rows = lambda v: v.reshape(depth, 1, -1).astype(F32)
    layer_params = (
        rows(norm1_g), (w_in * _in_proj_column_scale()).astype(BF16), rows(ret_gn_g),
        jnp.broadcast_to(conv_w.reshape(depth, CONV_WIDTH, 1, MIX).astype(F32),
                         (depth, CONV_WIDTH, SUBLANES, MIX)),
        rows(conv_b), rows(conv_ln_g), rows(conv_ln_b), rows(gmlp_ln_g), rows(gmlp_ln_b),
        gmlp_ws.astype(F32),
        jnp.broadcast_to(gmlp_bs.astype(F32)[..., None], (depth, GMLP_GROUPS, CHUNK, 128)),
        rows(jnp.tile(swa_q_g, (1, SWA_HEADS))) * (SWA_HEAD_DIM ** -0.5),
        rows(jnp.tile(swa_k_g, (1, SWA_KV_HEADS))),
        w_branch.astype(BF16), (w_out * 0.5).astype(BF16))
    ffn_scale = jnp.where(jnp.arange(2 * D_FF) < D_FF, 0.5, 1.0).astype(F32)
    n2g, wfi, wfo = rows(norm2_g), (w_ffn_in * ffn_scale).astype(BF16), w_ffn_out.astype(BF16)
    for l in range(depth):
        x = _mixer_call(l, x, tables, bucket, hsum, rel_bias.astype(F32), swa_sinks.astype(F32), layer_params)
        x = _ffn_call(l, x.reshape(B * S, D), n2g, wfi, wfo).reshape(B, S, D)
    return x
```

```python
import functools
import math

import jax
import jax.numpy as jnp
from jax import lax
from jax.experimental import pallas as pl
from jax.experimental.pallas import tpu as pltpu

D_MODEL = 1024
SEQ = 8192
EPS = 1e-6
MIX = 512
N_BRANCH = 4
CHUNK = 128
RET_HEADS = 4
RET_DK = 128
ROPE_BASE = 10000.0
CONV_WIDTH = 31
CONV_TAIL = 32
CONV_ROWS = 32
SUBLANES = 8
GMLP_GROUPS = 4
SWA_HEADS = 8
SWA_KV_HEADS = 2
SWA_GROUP = SWA_HEADS // SWA_KV_HEADS
SWA_HEAD_DIM = 64
SWA_WINDOW = 128
REL_BUCKETS = 32
REL_MAX_DIST = 128
D_FF = 2816
OFF_RET = 0
OFF_CONV = 2048
OFF_GMLP = 3072
OFF_SWA = 4096
OFF_GATE = 4864
IN_COLS = 8960

MIXER_TILE = 256
FFN_TILE = 512
COL_BLOCK = 256
VMEM_LIMIT_BYTES = 58 * 1024 * 1024

F32 = jnp.float32
BF16 = jnp.bfloat16


def _mm(a, b):
    return jnp.dot(a, b, preferred_element_type=F32)


def _sigmoid(x):
    return 0.5 * jnp.tanh(0.5 * x) + 0.5


def _silu(x):
    return x * _sigmoid(x)


def _silu_of_half(h):
    return h * jnp.tanh(h) + h


def _gelu_tanh(x):
    c = math.sqrt(2.0 / math.pi)
    return x * (0.5 * (1.0 + jnp.tanh(c * (x + 0.044715 * (x * x * x)))))


def _rms_norm(x, g):
    return x * lax.rsqrt(jnp.mean(x * x, axis=-1, keepdims=True) + EPS) * g


def _layer_norm(x, g, b):
    mu = jnp.mean(x, axis=-1, keepdims=True)
    d = x - mu
    var = jnp.mean(d * d, axis=-1, keepdims=True)
    return d * lax.rsqrt(var + EPS) * g + b


def _interleave(*streams):
    keyed = []
    for s, stream in enumerate(streams):
        keyed += [((i + 0.5) / len(stream), s, i, fn) for i, fn in enumerate(stream)]
    for _, _, _, fn in sorted(keyed, key=lambda e: e[:3]):
        fn()


def _mixer_kernel(layer, x_ref, cq_ref, sq_ref, ck_ref, sk_ref, decay_ref, xi_ref, zetat_ref, cdec_ref,
                  n1g_ref, win_ref, gng_ref, cw_ref, cb_ref, clg_ref, clb_ref, glg_ref, glb_ref,
                  ws_ref, bs_ref, qg_ref, kg_ref, sinks_ref, rel_ref, bucket_ref, hsum_ref, wbr_ref, wout_ref,
                  o_ref, state_ref, ubuf_ref, ushift_ref, yb_ref, kprev_ref, vprev_ref, bias_ref):
    ts = x_ref.shape[1]
    n_chunks = ts // CHUNK
    t = pl.program_id(1)

    @pl.when((pl.program_id(0) == 0) & (t == 0))
    def _():
        bucket = bucket_ref[...]
        for h in range(SWA_HEADS):
            bias_ref[h] = lax.fori_loop(
                0, REL_BUCKETS, lambda k, acc: jnp.where(bucket == k, rel_ref[k, h], acc),
                jnp.full(bucket.shape, -jnp.inf, F32))

    @pl.when(t == 0)
    def _():
        state_ref[...] = jnp.zeros_like(state_ref)
        ubuf_ref[0:CONV_TAIL, :] = jnp.zeros((CONV_TAIL, MIX), F32)
        kprev_ref[...] = jnp.zeros_like(kprev_ref)
        vprev_ref[...] = jnp.zeros_like(vprev_ref)

    x = x_ref[0]
    hb = _rms_norm(x, n1g_ref[...]).astype(BF16)
    z = {}

    def in_proj(off, width):
        return _mm(hb, win_ref[:, off:off + width])

    def proj_task(name, off, width):
        def run():
            z[name] = in_proj(off, width)
        return run

    blocks_per_branch = D_MODEL // COL_BLOCK
    mix = [None] * blocks_per_branch

    def branch_tasks(b, get_y):
        def task(j):
            def run():
                if j == 0:
                    z["ybf", b] = get_y().astype(BF16)
                c0 = j * COL_BLOCK
                gate = jnp.tanh(in_proj(OFF_GATE + b * D_MODEL + c0, COL_BLOCK))
                br = _mm(z["ybf", b], wbr_ref[b, :, c0:c0 + COL_BLOCK])
                term = gate * br + br
                mix[j] = term if mix[j] is None else mix[j] + term
            return run
        return [task(j) for j in range(blocks_per_branch)]

    zb = in_proj(OFF_CONV, 2 * MIX)
    ubuf_ref[CONV_TAIL:CONV_TAIL + ts, :] = zb[:, :MIX] * jnp.tanh(zb[:, MIX:]) + zb[:, :MIX]
    base = CONV_TAIL - (CONV_WIDTH - 1)
    taps = []
    shift_tasks = []
    for b in range(SUBLANES):
        n_a = (CONV_WIDTH - b + SUBLANES - 1) // SUBLANES
        off = base + b
        if off % SUBLANES == 0:
            src, src_off = ubuf_ref, off
        else:
            def shift(b=b, off=off, rows=ts + SUBLANES * (n_a - 1)):
                ushift_ref[b, 0:rows, :] = ubuf_ref[off:off + rows, :]
            shift_tasks.append(shift)
            src, src_off = ushift_ref.at[b], 0
        taps += [(SUBLANES * a + b, src, src_off + SUBLANES * a) for a in range(n_a)]

    _interleave(shift_tasks,
                [proj_task("qk", OFF_RET, 1024), proj_task("vg", OFF_RET + 1024, 1024),
                 proj_task("c", OFF_GMLP, 1024), proj_task("d", OFF_SWA, 768)])

    def gmlp_front():
        zc = _gelu_tanh(z["c"])
        z["ug"] = zc[:, :MIX]
        z["vg16"] = _layer_norm(zc[:, MIX:], glg_ref[...], glb_ref[...]).astype(BF16)

    def swa_front():
        zd = z["d"]
        qd, kd = zd[:, :512], zd[:, 512:640]
        hsum = hsum_ref[...]
        z["qn"] = qd * lax.rsqrt(_mm((qd * qd).astype(BF16), hsum) + EPS) * qg_ref[...]
        z["kn"] = kd * lax.rsqrt(_mm((kd * kd).astype(BF16), hsum[0:128, 0:128]) + EPS) * kg_ref[...]
        z["vd"] = zd[:, 640:768]

    gmlp_front()
    swa_front()

    cb, clg, clb = cb_ref[...], clg_ref[...], clb_ref[...]

    def conv_task(r0):
        def run():
            acc = cb
            for j, src, off in taps:
                w = jnp.concatenate([cw_ref[j]] * (CONV_ROWS // SUBLANES), axis=0)
                acc = acc + w * src[off + r0:off + r0 + CONV_ROWS, :]
            yb_ref[r0:r0 + CONV_ROWS, :] = _silu(_layer_norm(acc, clg, clb)).astype(BF16)
        return run

    for r0 in range(0, ts, CONV_ROWS):
        conv_task(r0)()
    ubuf_ref[0:CONV_TAIL, :] = ubuf_ref[ts:ts + CONV_TAIL, :]

    gng = gng_ref[...]
    cdec = cdec_ref[...]
    ya_chunks = []
    ret = {}

    def ret_scores_task(c):
        def run():
            r = slice(c * CHUNK, (c + 1) * CHUNK)
            st = state_ref[...]
            new_state = []
            for hd in range(RET_HEADS):
                l = slice(hd * RET_DK, (hd + 1) * RET_DK)
                q = z["qk"][r, hd * 128:(hd + 1) * 128]
                k = z["qk"][r, 512 + hd * 128:512 + (hd + 1) * 128]
                v = z["vg"][r, hd * 128:(hd + 1) * 128]
                q = q * cq_ref[r, :] + pltpu.roll(q, 64, 1) * sq_ref[r, :]
                k = k * ck_ref[r, :] + pltpu.roll(k, 64, 1) * sk_ref[r, :]
                vb = v.astype(BF16)
                kt = k.T
                s = _mm(q.astype(BF16), kt.astype(BF16)) * decay_ref[hd]
                ret[c, hd] = (jnp.concatenate([s.astype(BF16), (q * xi_ref[:, l]).astype(BF16)], axis=1),
                              jnp.concatenate([vb, st[:, l].astype(BF16)], axis=0))
                kzt = (kt * zetat_ref[hd]).astype(BF16)
                new_state.append(cdec[:, l] * st[:, l] + _mm(kzt, vb))
            state_ref[...] = jnp.concatenate(new_state, axis=1)
        return run

    def ret_output_task(c):
        def run():
            r = slice(c * CHUNK, (c + 1) * CHUNK)
            heads = []
            for hd in range(RET_HEADS):
                y = _mm(*ret[c, hd])
                mu = jnp.mean(y, axis=-1, keepdims=True)
                d = y - mu
                var = jnp.mean(d * d, axis=-1, keepdims=True)
                heads.append(d * lax.rsqrt(var + EPS))
            yn = jnp.concatenate(heads, axis=1) * gng
            ya_chunks.append(_silu_of_half(z["vg"][r, 512:1024]) * yn)
        return run

    tril = (lax.broadcasted_iota(jnp.int32, (CHUNK, CHUNK), 0)
            >= lax.broadcasted_iota(jnp.int32, (CHUNK, CHUNK), 1))
    s_groups = []

    def gmlp_task(g):
        def run():
            wsm = jnp.where(tril, ws_ref[g], 0.0).astype(BF16)
            vg = z["vg16"]
            rhs = jnp.concatenate([vg[c * CHUNK:(c + 1) * CHUNK, g * 128:(g + 1) * 128]
                                   for c in range(n_chunks)], axis=1)
            sg = _mm(wsm, rhs)
            bsg = bs_ref[g]
            s_groups.append(jnp.concatenate([sg[:, c * 128:(c + 1) * 128] + bsg
                                             for c in range(n_chunks)], axis=0))
        return run

    lane_lo = lax.broadcasted_iota(jnp.int32, (1, 128), 1) < SWA_HEAD_DIM
    first = jnp.where(t == 0, -jnp.inf, 0.0)
    yd_chunks = []
    att = {}

    def attn_scores_task(c):
        def run():
            r = slice(c * CHUNK, (c + 1) * CHUNK)
            kc, vc = z["kn"][r, :], z["vd"][r, :]
            kcat = jnp.concatenate([kprev_ref[...], kc], axis=0)
            vcat = jnp.concatenate([vprev_ref[...], vc], axis=0)
            kprev_ref[...] = kc
            vprev_ref[...] = vc
            vroll = pltpu.roll(vcat, 64, 1)
            kt = kcat.T
            kt0, kt1 = kt[0:SWA_HEAD_DIM, :], kt[SWA_HEAD_DIM:2 * SWA_HEAD_DIM, :]
            zero = jnp.zeros_like(kt0)
            kplace = [[jnp.concatenate([kt0, zero], axis=0).astype(BF16),
                       jnp.concatenate([zero, kt0], axis=0).astype(BF16)],
                      [jnp.concatenate([kt1, zero], axis=0).astype(BF16),
                       jnp.concatenate([zero, kt1], axis=0).astype(BF16)]]
            att["v", c] = [[jnp.where(lane_lo, vcat, 0.0).astype(BF16), jnp.where(lane_lo, 0.0, vroll).astype(BF16)],
                           [jnp.where(lane_lo, vroll, 0.0).astype(BF16), jnp.where(lane_lo, 0.0, vcat).astype(BF16)]]
            for hidx in range(SWA_HEADS):
                j, pos = hidx // 2, hidx % 2
                kv = hidx // SWA_GROUP
                qblk = z["qn"][r, j * 128:(j + 1) * 128].astype(BF16)
                s = _mm(qblk, kplace[kv][pos]) + bias_ref[hidx]
                if c == 0:
                    s = jnp.concatenate([s[:, :CHUNK] + first, s[:, CHUNK:]], axis=1)
                att["s", c, hidx] = s
        return run

    def attn_softmax_task(c):
        def run():
            for hidx in range(SWA_HEADS):
                s = att["s", c, hidx]
                sink = sinks_ref[layer, hidx]
                m = jnp.maximum(jnp.max(s, axis=-1, keepdims=True), sink)
                p = jnp.exp(s - m)
                den = jnp.sum(p, axis=-1, keepdims=True) + jnp.exp(sink - m)
                att["p", c, hidx] = (p.astype(BF16), 1.0 / den)
        return run

    def attn_output_task(c):
        def run():
            outs = []
            for j in range(SWA_HEADS // 2):
                kv = (2 * j) // SWA_GROUP
                (p0, r0), (p1, r1) = att["p", c, 2 * j], att["p", c, 2 * j + 1]
                vplace = att["v", c][kv]
                o = _mm(jnp.concatenate([p0, p1], axis=1),
                        jnp.concatenate([vplace[0], vplace[1]], axis=0))
                outs.append(o * jnp.where(lane_lo, r0, r1))
            yd_chunks.append(jnp.concatenate(outs, axis=1))
        return run

    chunks = range(n_chunks)
    _interleave([ret_scores_task(c) for c in chunks] + [attn_scores_task(c) for c in chunks]
                + [ret_output_task(c) for c in chunks],
                [gmlp_task(g) for g in range(GMLP_GROUPS)])
    _interleave([attn_softmax_task(c) for c in chunks] + [attn_output_task(c) for c in chunks],
                branch_tasks(0, lambda: jnp.concatenate(ya_chunks, axis=0))
                + branch_tasks(1, lambda: yb_ref[...])
                + branch_tasks(2, lambda: z["ug"] * jnp.concatenate(s_groups, axis=1)))
    for task in branch_tasks(3, lambda: jnp.concatenate(yd_chunks, axis=0)):
        task()

    o_ref[0] = x + _mm(jnp.concatenate(mix, axis=1).astype(BF16), wout_ref[...])


def _ffn_kernel(x_ref, g_ref, win_ref, wout_ref, o_ref):
    x = x_ref[...]
    hb = _rms_norm(x, g_ref[...]).astype(BF16)
    gh = _mm(hb, win_ref[:, :D_FF])
    up = _mm(hb, win_ref[:, D_FF:])
    o_ref[...] = x + _mm((_silu_of_half(gh) * up).astype(BF16), wout_ref[...])


def _const_spec(shape):
    nd = len(shape)
    return pl.BlockSpec(shape, lambda *_: (0,) * nd, pipeline_mode=pl.Buffered(1))


def _layer_spec(arr, layer):
    nd = arr.ndim - 1
    return pl.BlockSpec((None,) + arr.shape[1:], lambda *_: (layer,) + (0,) * nd, pipeline_mode=pl.Buffered(1))


def _t5_bucket(dist):
    max_exact = REL_BUCKETS // 2
    d = jnp.maximum(dist, 1).astype(F32)
    large = max_exact + (jnp.log(d / max_exact) / math.log(REL_MAX_DIST / max_exact)
                         * (REL_BUCKETS - max_exact)).astype(jnp.int32)
    large = jnp.minimum(large, REL_BUCKETS - 1)
    return jnp.where(dist < max_exact, dist, large)


def _position_tables():
    half = RET_DK // 2
    inv = ROPE_BASE ** (-jnp.arange(half, dtype=F32) / half)
    ang = jnp.arange(SEQ).astype(F32)[:, None] * inv[None, :]
    cos, sin = jnp.cos(ang), jnp.sin(ang)
    cos2 = jnp.concatenate([cos, cos], axis=1)
    sin2 = jnp.concatenate([-sin, sin], axis=1)
    kscale = RET_DK ** -0.5
    gamma = 1.0 - 2.0 ** (-5.0 - jnp.arange(RET_HEADS, dtype=F32))
    log_g = jnp.log(gamma)
    idx = jnp.arange(CHUNK, dtype=F32)
    diff = idx[:, None] - idx[None, :]
    decay = jnp.where(diff >= 0, jnp.exp(log_g[:, None, None] * jnp.maximum(diff, 0.0)), 0.0)
    xi = jnp.repeat(jnp.exp(log_g[None, :] * (idx[:, None] + 1.0)), RET_DK, axis=1)
    zeta = jnp.exp(log_g[None, :] * (CHUNK - 1.0 - idx[:, None]))
    zetat = jnp.broadcast_to(zeta.T[:, None, :], (RET_HEADS, RET_DK, CHUNK))
    cdec = jnp.repeat(jnp.exp(log_g * CHUNK), RET_DK)[None, :]
    return cos2, sin2, cos2 * kscale, sin2 * kscale, decay, xi, zetat, cdec


def _bucket_table():
    qi = jnp.arange(CHUNK)[:, None] + CHUNK
    kj = jnp.arange(2 * CHUNK)[None, :]
    dist = qi - kj
    in_win = (dist >= 0) & (dist < SWA_WINDOW)
    return jnp.where(in_win, _t5_bucket(jnp.maximum(dist, 0)), -1).astype(jnp.int32)


def _in_proj_column_scale():
    col = jnp.arange(IN_COLS)
    halved = (((col >= OFF_CONV) & (col < OFF_GMLP))
              | ((col >= OFF_RET + 3 * MIX) & (col < OFF_CONV))
              | (col >= OFF_GATE))
    return jnp.where(halved, 0.5, 1.0).astype(F32)


def _mixer_call(layer, x, tables, bucket, hsum, rel_bias, sinks, layer_params):
    B, S, D = x.shape
    ts = MIXER_TILE
    pos_spec = pl.BlockSpec((ts, 128), lambda b, t: (t, 0))
    smem = pl.BlockSpec(memory_space=pltpu.SMEM)
    n1g, win, gng, cw, cb, clg, clb, glg, glb, ws, bs, qg, kg, wbr, wout = layer_params
    lspec = lambda a: _layer_spec(a, layer)
    in_specs = (
        [pl.BlockSpec((1, ts, D), lambda b, t: (b, t, 0))]
        + [pos_spec] * 4 + [_const_spec(a.shape) for a in tables[4:]]
        + [lspec(a) for a in (n1g, win, gng, cw, cb, clg, clb, glg, glb, ws, bs, qg, kg)]
        + [smem, smem, _const_spec(bucket.shape), _const_spec(hsum.shape), lspec(wbr), lspec(wout)])
    return pl.pallas_call(
        functools.partial(_mixer_kernel, layer),
        grid=(B, S // ts),
        in_specs=in_specs,
        out_specs=pl.BlockSpec((1, ts, D), lambda b, t: (b, t, 0)),
        out_shape=jax.ShapeDtypeStruct(x.shape, x.dtype),
        scratch_shapes=[
            pltpu.VMEM((RET_DK, MIX), F32),
            pltpu.VMEM((CONV_TAIL + ts, MIX), F32),
            pltpu.VMEM((SUBLANES, ts + CONV_TAIL - SUBLANES, MIX), F32),
            pltpu.VMEM((ts, MIX), BF16),
            pltpu.VMEM((CHUNK, 128), F32),
            pltpu.VMEM((CHUNK, 128), F32),
            pltpu.VMEM((SWA_HEADS, CHUNK, 2 * CHUNK), F32),
        ],
        compiler_params=pltpu.CompilerParams(
            dimension_semantics=("arbitrary", "arbitrary"),
            vmem_limit_bytes=VMEM_LIMIT_BYTES),
        name="mixer",
    )(x, *tables, n1g, win, gng, cw, cb, clg, clb, glg, glb, ws, bs, qg, kg,
      sinks, rel_bias, bucket, hsum, wbr, wout)


def _ffn_call(layer, x2, g, win, wout):
    T, D = x2.shape
    ts = FFN_TILE
    return pl.pallas_call(
        _ffn_kernel,
        grid=(T // ts,),
        in_specs=[pl.BlockSpec((ts, D), lambda i: (i, 0)),
                  _layer_spec(g, layer), _layer_spec(win, layer), _layer_spec(wout, layer)],
        out_specs=pl.BlockSpec((ts, D), lambda i: (i, 0)),
        out_shape=jax.ShapeDtypeStruct(x2.shape, x2.dtype),
        compiler_params=pltpu.CompilerParams(
            dimension_semantics=("arbitrary",),
            vmem_limit_bytes=VMEM_LIMIT_BYTES),
        name="ffn",
    )(x2, g, win, wout)


def kernel(x, rel_bias, norm1_g, w_in, ret_gn_g, conv_w, conv_b, conv_ln_g, conv_ln_b, gmlp_ln_g, gmlp_ln_b, gmlp_ws, gmlp_bs, swa_q_g, swa_k_g, swa_sinks, w_branch, w_out, norm2_g, w_ffn_in, w_ffn_out):
    B, S, D = x.shape
    depth = w_in.shape[0]
    tables = _position_tables()
    bucket = _bucket_table()
    head_id = jnp.arange(MIX) // SWA_HEAD_DIM
    hsum = jnp.where(head_id[:, None] == head_id[None, :], 1.0 / SWA_HEAD_DIM, 0.0).astype(BF16)
    rows = lambda v: v.reshape(depth, 1, -1).astype(F32)
    layer_params = (
        rows(norm1_g), (w_in * _in_proj_column_scale()).astype(BF16), rows(ret_gn_g),
        jnp.broadcast_to(conv_w.reshape(depth, CONV_WIDTH, 1, MIX).astype(F32),
                         (depth, CONV_WIDTH, SUBLANES, MIX)),
        rows(conv_b), rows(conv_ln_g), rows(conv_ln_b), rows(gmlp_ln_g), rows(gmlp_ln_b),
        gmlp_ws.astype(F32),
        jnp.broadcast_to(gmlp_bs.astype(F32)[..., None], (depth, GMLP_GROUPS, CHUNK, 128)),
        rows(jnp.tile(swa_q_g, (1, SWA_HEADS))) * (SWA_HEAD_DIM ** -0.5),
        rows(jnp.tile(swa_k_g, (1, SWA_KV_HEADS))),
        w_branch.astype(BF16), (w_out * 0.5).astype(BF16))
    ffn_scale = jnp.where(jnp.arange(2 * D_FF) < D_FF, 0.5, 1.0).astype(F32)
    n2g, wfi, wfo = rows(norm2_g), (w_ffn_in * ffn_scale).astype(BF16), w_ffn_out.astype(BF16)
    for l in range(depth):
        x = _mixer_call(l, x, tables, bucket, hsum, rel_bias.astype(F32), swa_sinks.astype(F32), layer_params)
        x = _ffn_call(l, x.reshape(B * S, D), n2g, wfi, wfo).reshape(B, S, D)
    return x
```

```python
import functools
import math

import jax
import jax.numpy as jnp
from jax import lax
from jax.experimental import pallas as pl
from jax.experimental.pallas import tpu as pltpu

D_MODEL = 1024
SEQ = 8192
EPS = 1e-6
MIX = 512
N_BRANCH = 4
CHUNK = 128
RET_HEADS = 4
RET_DK = 128
ROPE_BASE = 10000.0
CONV_WIDTH = 31
CONV_TAIL = 32
CONV_ROWS = 32
SUBLANES = 8
GMLP_GROUPS = 4
SWA_HEADS = 8
SWA_KV_HEADS = 2
SWA_GROUP = SWA_HEADS // SWA_KV_HEADS
SWA_HEAD_DIM = 64
SWA_WINDOW = 128
REL_BUCKETS = 32
REL_MAX_DIST = 128
D_FF = 2816
OFF_RET = 0
OFF_CONV = 2048
OFF_GMLP = 3072
OFF_SWA = 4096
OFF_GATE = 4864
IN_COLS = 8960

MIXER_TILE = 512
FFN_TILE = 512
COL_BLOCK = 256
VMEM_LIMIT_BYTES = 58 * 1024 * 1024

F32 = jnp.float32
BF16 = jnp.bfloat16


def _mm(a, b):
    return jnp.dot(a, b, preferred_element_type=F32)


def _sigmoid(x):
    return 0.5 * jnp.tanh(0.5 * x) + 0.5


def _silu(x):
    return x * _sigmoid(x)


def _silu_of_half(h):
    return h * jnp.tanh(h) + h


def _gelu_tanh(x):
    c = math.sqrt(2.0 / math.pi)
    return x * (0.5 * (1.0 + jnp.tanh(c * (x + 0.044715 * (x * x * x)))))


def _rms_norm(x, g):
    return x * lax.rsqrt(jnp.mean(x * x, axis=-1, keepdims=True) + EPS) * g


def _layer_norm(x, g, b):
    mu = jnp.mean(x, axis=-1, keepdims=True)
    d = x - mu
    var = jnp.mean(d * d, axis=-1, keepdims=True)
    return d * lax.rsqrt(var + EPS) * g + b


def _interleave(*streams):
    keyed = []
    for s, stream in enumerate(streams):
        keyed += [((i + 0.5) / len(stream), s, i, fn) for i, fn in enumerate(stream)]
    for _, _, _, fn in sorted(keyed, key=lambda e: e[:3]):
        fn()


def _mixer_kernel(layer, x_ref, cq_ref, sq_ref, ck_ref, sk_ref, decay_ref, xi_ref, zetat_ref, cdec_ref,
                  n1g_ref, win_ref, gng_ref, cw_ref, cb_ref, clg_ref, clb_ref, glg_ref, glb_ref,
                  ws_ref, bs_ref, qg_ref, kg_ref, sinks_ref, rel_ref, bucket_ref, hsum_ref, wbr_ref, wout_ref,
                  o_ref, state_ref, ubuf_ref, ushift_ref, yb_ref, kprev_ref, vprev_ref, bias_ref):
    ts = x_ref.shape[1]
    n_chunks = ts // CHUNK
    t = pl.program_id(1)

    @pl.when((pl.program_id(0) == 0) & (t == 0))
    def _():
        bucket = bucket_ref[...]
        for h in range(SWA_HEADS):
            bias_ref[h] = lax.fori_loop(
                0, REL_BUCKETS, lambda k, acc: jnp.where(bucket == k, rel_ref[k, h], acc),
                jnp.full(bucket.shape, -jnp.inf, F32))

    @pl.when(t == 0)
    def _():
        state_ref[...] = jnp.zeros_like(state_ref)
        ubuf_ref[0:CONV_TAIL, :] = jnp.zeros((CONV_TAIL, MIX), F32)
        kprev_ref[...] = jnp.zeros_like(kprev_ref)
        vprev_ref[...] = jnp.zeros_like(vprev_ref)

    x = x_ref[0]
    hb = _rms_norm(x, n1g_ref[...]).astype(BF16)
    z = {}

    def in_proj(off, width):
        return _mm(hb, win_ref[:, off:off + width])

    def proj_task(name, off, width):
        def run():
            z[name] = in_proj(off, width)
        return run

    blocks_per_branch = D_MODEL // COL_BLOCK
    mix = [None] * blocks_per_branch

    def branch_tasks(b, get_y):
        def task(j):
            def run():
                if j == 0:
                    z["ybf", b] = get_y().astype(BF16)
                c0 = j * COL_BLOCK
                gate = jnp.tanh(in_proj(OFF_GATE + b * D_MODEL + c0, COL_BLOCK))
                br = _mm(z["ybf", b], wbr_ref[b, :, c0:c0 + COL_BLOCK])
                term = gate * br + br
                mix[j] = term if mix[j] is None else mix[j] + term
            return run
        return [task(j) for j in range(blocks_per_branch)]

    zb = in_proj(OFF_CONV, 2 * MIX)
    ubuf_ref[CONV_TAIL:CONV_TAIL + ts, :] = zb[:, :MIX] * jnp.tanh(zb[:, MIX:]) + zb[:, :MIX]
    base = CONV_TAIL - (CONV_WIDTH - 1)
    taps = []
    shift_tasks = []
    for b in range(SUBLANES):
        n_a = (CONV_WIDTH - b + SUBLANES - 1) // SUBLANES
        off = base + b
        if off % SUBLANES == 0:
            src, src_off = ubuf_ref, off
        else:
            def shift(b=b, off=off, rows=ts + SUBLANES * (n_a - 1)):
                ushift_ref[b, 0:rows, :] = ubuf_ref[off:off + rows, :]
            shift_tasks.append(shift)
            src, src_off = ushift_ref.at[b], 0
        taps += [(SUBLANES * a + b, src, src_off + SUBLANES * a) for a in range(n_a)]

    _interleave(shift_tasks,
                [proj_task("qk", OFF_RET, 1024), proj_task("vg", OFF_RET + 1024, 1024),
                 proj_task("c", OFF_GMLP, 1024), proj_task("d", OFF_SWA, 768)])

    def gmlp_front():
        zc = _gelu_tanh(z["c"])
        z["ug"] = zc[:, :MIX]
        z["vg16"] = _layer_norm(zc[:, MIX:], glg_ref[...], glb_ref[...]).astype(BF16)

    def swa_front():
        zd = z["d"]
        qd, kd = zd[:, :512], zd[:, 512:640]
        hsum = hsum_ref[...]
        z["qn"] = qd * lax.rsqrt(_mm((qd * qd).astype(BF16), hsum) + EPS) * qg_ref[...]
        z["kn"] = kd * lax.rsqrt(_mm((kd * kd).astype(BF16), hsum[0:128, 0:128]) + EPS) * kg_ref[...]
        z["vd"] = zd[:, 640:768]

    gmlp_front()
    swa_front()

    cb, clg, clb = cb_ref[...], clg_ref[...], clb_ref[...]

    def conv_task(r0):
        def run():
            acc = cb
            for j, src, off in taps:
                w = jnp.concatenate([cw_ref[j]] * (CONV_ROWS // SUBLANES), axis=0)
                acc = acc + w * src[off + r0:off + r0 + CONV_ROWS, :]
            yb_ref[r0:r0 + CONV_ROWS, :] = _silu(_layer_norm(acc, clg, clb)).astype(BF16)
        return run

    for r0 in range(0, ts, CONV_ROWS):
        conv_task(r0)()
    ubuf_ref[0:CONV_TAIL, :] = ubuf_ref[ts:ts + CONV_TAIL, :]

    gng = gng_ref[...]
    cdec = cdec_ref[...]
    ya_chunks = []
    ret = {}

    def ret_scores_task(c):
        def run():
            r = slice(c * CHUNK, (c + 1) * CHUNK)
            st = state_ref[...]
            new_state = []
            for hd in range(RET_HEADS):
                l = slice(hd * RET_DK, (hd + 1) * RET_DK)
                q = z["qk"][r, hd * 128:(hd + 1) * 128]
                k = z["qk"][r, 512 + hd * 128:512 + (hd + 1) * 128]
                v = z["vg"][r, hd * 128:(hd + 1) * 128]
                q = q * cq_ref[r, :] + pltpu.roll(q, 64, 1) * sq_ref[r, :]
                k = k * ck_ref[r, :] + pltpu.roll(k, 64, 1) * sk_ref[r, :]
                vb = v.astype(BF16)
                kt = k.T
                s = _mm(q.astype(BF16), kt.astype(BF16)) * decay_ref[hd]
                ret[c, hd] = (jnp.concatenate([s.astype(BF16), (q * xi_ref[:, l]).astype(BF16)], axis=1),
                              jnp.concatenate([vb, st[:, l].astype(BF16)], axis=0))
                kzt = (kt * zetat_ref[hd]).astype(BF16)
                new_state.append(cdec[:, l] * st[:, l] + _mm(kzt, vb))
            state_ref[...] = jnp.concatenate(new_state, axis=1)
        return run

    def ret_output_task(c):
        def run():
            r = slice(c * CHUNK, (c + 1) * CHUNK)
            heads = []
            for hd in range(RET_HEADS):
                y = _mm(*ret[c, hd])
                mu = jnp.mean(y, axis=-1, keepdims=True)
                d = y - mu
                var = jnp.mean(d * d, axis=-1, keepdims=True)
                heads.append(d * lax.rsqrt(var + EPS))
            yn = jnp.concatenate(heads, axis=1) * gng
            ya_chunks.append(_silu_of_half(z["vg"][r, 512:1024]) * yn)
        return run

    tril = (lax.broadcasted_iota(jnp.int32, (CHUNK, CHUNK), 0)
            >= lax.broadcasted_iota(jnp.int32, (CHUNK, CHUNK), 1))
    s_groups = []

    def gmlp_task(g):
        def run():
            wsm = jnp.where(tril, ws_ref[g], 0.0).astype(BF16)
            vg = z["vg16"]
            rhs = jnp.concatenate([vg[c * CHUNK:(c + 1) * CHUNK, g * 128:(g + 1) * 128]
                                   for c in range(n_chunks)], axis=1)
            sg = _mm(wsm, rhs)
            bsg = bs_ref[g]
            s_groups.append(jnp.concatenate([sg[:, c * 128:(c + 1) * 128] + bsg
                                             for c in range(n_chunks)], axis=0))
        return run

    lane_lo = lax.broadcasted_iota(jnp.int32, (1, 128), 1) < SWA_HEAD_DIM
    first = jnp.where(t == 0, -jnp.inf, 0.0)
    yd_chunks = []
    att = {}

    def attn_scores_task(c):
        def run():
            r = slice(c * CHUNK, (c + 1) * CHUNK)
            kc, vc = z["kn"][r, :], z["vd"][r, :]
            kcat = jnp.concatenate([kprev_ref[...], kc], axis=0)
            vcat = jnp.concatenate([vprev_ref[...], vc], axis=0)
            kprev_ref[...] = kc
            vprev_ref[...] = vc
            vroll = pltpu.roll(vcat, 64, 1)
            kt = kcat.T
            kt0, kt1 = kt[0:SWA_HEAD_DIM, :], kt[SWA_HEAD_DIM:2 * SWA_HEAD_DIM, :]
            zero = jnp.zeros_like(kt0)
            kplace = [[jnp.concatenate([kt0, zero], axis=0).astype(BF16),
                       jnp.concatenate([zero, kt0], axis=0).astype(BF16)],
                      [jnp.concatenate([kt1, zero], axis=0).astype(BF16),
                       jnp.concatenate([zero, kt1], axis=0).astype(BF16)]]
            att["v", c] = [[jnp.where(lane_lo, vcat, 0.0).astype(BF16), jnp.where(lane_lo, 0.0, vroll).astype(BF16)],
                           [jnp.where(lane_lo, vroll, 0.0).astype(BF16), jnp.where(lane_lo, 0.0, vcat).astype(BF16)]]
            for hidx in range(SWA_HEADS):
                j, pos = hidx // 2, hidx % 2
                kv = hidx // SWA_GROUP
                qblk = z["qn"][r, j * 128:(j + 1) * 128].astype(BF16)
                s = _mm(qblk, kplace[kv][pos]) + bias_ref[hidx]
                if c == 0:
                    s = jnp.concatenate([s[:, :CHUNK] + first, s[:, CHUNK:]], axis=1)
                att["s", c, hidx] = s
        return run

    def attn_softmax_task(c):
        def run():
            for hidx in range(SWA_HEADS):
                s = att["s", c, hidx]
                sink = sinks_ref[layer, hidx]
                m = jnp.maximum(jnp.max(s, axis=-1, keepdims=True), sink)
                p = jnp.exp(s - m)
                den = jnp.sum(p, axis=-1, keepdims=True) + jnp.exp(sink - m)
                att["p", c, hidx] = (p.astype(BF16), 1.0 / den)
        return run

    def attn_output_task(c):
        def run():
            outs = []
            for j in range(SWA_HEADS // 2):
                kv = (2 * j) // SWA_GROUP
                (p0, r0), (p1, r1) = att["p", c, 2 * j], att["p", c, 2 * j + 1]
                vplace = att["v", c][kv]
                o = _mm(jnp.concatenate([p0, p1], axis=1),
                        jnp.concatenate([vplace[0], vplace[1]], axis=0))
                outs.append(o * jnp.where(lane_lo, r0, r1))
            yd_chunks.append(jnp.concatenate(outs, axis=1))
        return run

    chunks = range(n_chunks)
    _interleave([ret_scores_task(c) for c in chunks] + [attn_scores_task(c) for c in chunks]
                + [ret_output_task(c) for c in chunks],
                [gmlp_task(g) for g in range(GMLP_GROUPS)])
    _interleave([attn_softmax_task(c) for c in chunks] + [attn_output_task(c) for c in chunks],
                branch_tasks(0, lambda: jnp.concatenate(ya_chunks, axis=0))
                + branch_tasks(1, lambda: yb_ref[...])
                + branch_tasks(2, lambda: z["ug"] * jnp.concatenate(s_groups, axis=1)))
    for task in branch_tasks(3, lambda: jnp.concatenate(yd_chunks, axis=0)):
        task()

    o_ref[0] = x + _mm(jnp.concatenate(mix, axis=1).astype(BF16), wout_ref[...])


def _ffn_kernel(x_ref, g_ref, win_ref, wout_ref, o_ref):
    x = x_ref[...]
    hb = _rms_norm(x, g_ref[...]).astype(BF16)
    gh = _mm(hb, win_ref[:, :D_FF])
    up = _mm(hb, win_ref[:, D_FF:])
    o_ref[...] = x + _mm((_silu_of_half(gh) * up).astype(BF16), wout_ref[...])


def _const_spec(shape):
    nd = len(shape)
    return pl.BlockSpec(shape, lambda *_: (0,) * nd, pipeline_mode=pl.Buffered(1))


def _layer_spec(arr, layer):
    nd = arr.ndim - 1
    return pl.BlockSpec((None,) + arr.shape[1:], lambda *_: (layer,) + (0,) * nd, pipeline_mode=pl.Buffered(1))


def _t5_bucket(dist):
    max_exact = REL_BUCKETS // 2
    d = jnp.maximum(dist, 1).astype(F32)
    large = max_exact + (jnp.log(d / max_exact) / math.log(REL_MAX_DIST / max_exact)
                         * (REL_BUCKETS - max_exact)).astype(jnp.int32)
    large = jnp.minimum(large, REL_BUCKETS - 1)
    return jnp.where(dist < max_exact, dist, large)


def _position_tables():
    half = RET_DK // 2
    inv = ROPE_BASE ** (-jnp.arange(half, dtype=F32) / half)
    ang = jnp.arange(SEQ).astype(F32)[:, None] * inv[None, :]
    cos, sin = jnp.cos(ang), jnp.sin(ang)
    cos2 = jnp.concatenate([cos, cos], axis=1)
    sin2 = jnp.concatenate([-sin, sin], axis=1)
    kscale = RET_DK ** -0.5
    gamma = 1.0 - 2.0 ** (-5.0 - jnp.arange(RET_HEADS, dtype=F32))
    log_g = jnp.log(gamma)
    idx = jnp.arange(CHUNK, dtype=F32)
    diff = idx[:, None] - idx[None, :]
    decay = jnp.where(diff >= 0, jnp.exp(log_g[:, None, None] * jnp.maximum(diff, 0.0)), 0.0)
    xi = jnp.repeat(jnp.exp(log_g[None, :] * (idx[:, None] + 1.0)), RET_DK, axis=1)
    zeta = jnp.exp(log_g[None, :] * (CHUNK - 1.0 - idx[:, None]))
    zetat = jnp.broadcast_to(zeta.T[:, None, :], (RET_HEADS, RET_DK, CHUNK))
    cdec = jnp.repeat(jnp.exp(log_g * CHUNK), RET_DK)[None, :]
    return cos2, sin2, cos2 * kscale, sin2 * kscale, decay, xi, zetat, cdec


def _bucket_table():
    qi = jnp.arange(CHUNK)[:, None] + CHUNK
    kj = jnp.arange(2 * CHUNK)[None, :]
    dist = qi - kj
    in_win = (dist >= 0) & (dist < SWA_WINDOW)
    return jnp.where(in_win, _t5_bucket(jnp.maximum(dist, 0)), -1).astype(jnp.int32)


def _in_proj_column_scale():
    col = jnp.arange(IN_COLS)
    halved = (((col >= OFF_CONV) & (col < OFF_GMLP))
              | ((col >= OFF_RET + 3 * MIX) & (col < OFF_CONV))
              | (col >= OFF_GATE))
    return jnp.where(halved, 0.5, 1.0).astype(F32)


def _mixer_call(layer, x, tables, bucket, hsum, rel_bias, sinks, layer_params):
    B, S, D = x.shape
    ts = MIXER_TILE
    pos_spec = pl.BlockSpec((ts, 128), lambda b, t: (t, 0))
    smem = pl.BlockSpec(memory_space=pltpu.SMEM)
    n1g, win, gng, cw, cb, clg, clb, glg, glb, ws, bs, qg, kg, wbr, wout = layer_params
    lspec = lambda a: _layer_spec(a, layer)
    in_specs = (
        [pl.BlockSpec((1, ts, D), lambda b, t: (b, t, 0))]
        + [pos_spec] * 4 + [_const_spec(a.shape) for a in tables[4:]]
        + [lspec(a) for a in (n1g, win, gng, cw, cb, clg, clb, glg, glb, ws, bs, qg, kg)]
        + [smem, smem, _const_spec(bucket.shape), _const_spec(hsum.shape), lspec(wbr), lspec(wout)])
    return pl.pallas_call(
        functools.partial(_mixer_kernel, layer),
        grid=(B, S // ts),
        in_specs=in_specs,
        out_specs=pl.BlockSpec((1, ts, D), lambda b, t: (b, t, 0)),
        out_shape=jax.ShapeDtypeStruct(x.shape, x.dtype),
        scratch_shapes=[
            pltpu.VMEM((RET_DK, MIX), F32),
            pltpu.VMEM((CONV_TAIL + ts, MIX), F32),
            pltpu.VMEM((SUBLANES, ts + CONV_TAIL - SUBLANES, MIX), F32),
            pltpu.VMEM((ts, MIX), BF16),
            pltpu.VMEM((CHUNK, 128), F32),
            pltpu.VMEM((CHUNK, 128), F32),
            pltpu.VMEM((SWA_HEADS, CHUNK, 2 * CHUNK), F32),
        ],
        compiler_params=pltpu.CompilerParams(
            dimension_semantics=("arbitrary", "arbitrary"),
            vmem_limit_bytes=VMEM_LIMIT_BYTES),
        name="mixer",
    )(x, *tables, n1g, win, gng, cw, cb, clg, clb, glg, glb, ws, bs, qg, kg,
      sinks, rel_bias, bucket, hsum, wbr, wout)


def _ffn_call(layer, x2, g, win, wout):
    T, D = x2.shape
    ts = FFN_TILE
    return pl.pallas_call(
        _ffn_kernel,
        grid=(T // ts,),
        in_specs=[pl.BlockSpec((ts, D), lambda i: (i, 0)),
                  _layer_spec(g, layer), _layer_spec(win, layer), _layer_spec(wout, layer)],
        out_specs=pl.BlockSpec((ts, D), lambda i: (i, 0)),
        out_shape=jax.ShapeDtypeStruct(x2.shape, x2.dtype),
        compiler_params=pltpu.CompilerParams(
            dimension_semantics=("arbitrary",),
            vmem_limit_bytes=VMEM_LIMIT_BYTES),
        name="ffn",
    )(x2, g, win, wout)


def kernel(x, rel_bias, norm1_g, w_in, ret_gn_g, conv_w, conv_b, conv_ln_g, conv_ln_b, gmlp_ln_g, gmlp_ln_b, gmlp_ws, gmlp_bs, swa_q_g, swa_k_g, swa_sinks, w_branch, w_out, norm2_g, w_ffn_in, w_ffn_out):
    B, S, D = x.shape
    depth = w_in.shape[0]
    tables = _position_tables()
    bucket = _bucket_table()
    head_id = jnp.arange(MIX) // SWA_HEAD_DIM
    hsum = jnp.where(head_id[:, None] == head_id[None, :], 1.0 / SWA_HEAD_DIM, 0.0).astype(BF16)
    rows = lambda v: v.reshape(depth, 1, -1).astype(F32)
    layer_params = (
        rows(norm1_g), (w_in * _in_proj_column_scale()).astype(BF16), rows(ret_gn_g),
        jnp.broadcast_to(conv_w.reshape(depth, CONV_WIDTH, 1, MIX).astype(F32),
                         (depth, CONV_WIDTH, SUBLANES, MIX)),
        rows(conv_b), rows(conv_ln_g), rows(conv_ln_b), rows(gmlp_ln_g), rows(gmlp_ln_b),
        gmlp_ws.astype(F32),
        jnp.broadcast_to(gmlp_bs.astype(F32)[..., None], (depth, GMLP_GROUPS, CHUNK, 128)),
        rows(jnp.tile(swa_q_g, (1, SWA_HEADS))) * (SWA_HEAD_DIM ** -0.5),
        rows(jnp.tile(swa_k_g, (1, SWA_KV_HEADS))),
        w_branch.astype(BF16), (w_out * 0.5).astype(BF16))
    ffn_scale = jnp.where(jnp.arange(2 * D_FF) < D_FF, 0.5, 1.0).astype(F32)
    n2g, wfi, wfo = rows(norm2_g), (w_ffn_in * ffn_scale).astype(BF16), w_ffn_out.astype(BF16)
    for l in range(depth):
        x = _mixer_call(l, x, tables, bucket, hsum, rel_bias.astype(F32), swa_sinks.astype(F32), layer_params)
        x = _ffn_call(l, x.reshape(B * S, D), n2g, wfi, wfo).reshape(B, S, D)
    return x
```
